```python
import jax, jax.numpy as jnp
from jax import lax
import numpy as np

D_MODEL = 1024
BATCH = 4
SEQ = 8192
DEPTH = 2

PLE_DIM = 256
D_MIX = D_MODEL
MLA_HEADS = 8
MLA_NOPE = 64
MLA_ROPE = 32
MLA_V = 64
Q_LORA = 256
KV_LORA = 128
DIL_HEADS = 8
DIL_HEAD_DIM = 64
DIL_CONFIGS = ((128, 1), (512, 4), (2048, 16))
D_FF = 2816
MACARON = 0.5
ROPE_THETA = 10000.0
EPS = 1e-6
Q_BLOCK = 128
NEG = -1e30
N_NORMS = 8

MLA_WIDTH = MLA_HEADS * MLA_V
DIL_WIDTH = DIL_HEADS * DIL_HEAD_DIM
N_IN = Q_LORA + KV_LORA + MLA_ROPE + 3 * DIL_WIDTH

kernel_name = "hybrid_mla_dilated_macaron_block"


def rmsnorm(x, g):
    xf = x.astype(jnp.float32)
    y = xf * lax.rsqrt(jnp.mean(xf * xf, axis=-1, keepdims=True) + EPS)
    return (y * g.astype(jnp.float32)).astype(x.dtype)


def rope_tables(positions, dim, dtype):
    inv = ROPE_THETA ** (-jnp.arange(0, dim, 2, dtype=jnp.float32) / dim)
    ang = positions.astype(jnp.float32)[..., None] * inv
    return jnp.cos(ang)[:, :, None, :].astype(dtype), jnp.sin(ang)[:, :, None, :].astype(dtype)


def apply_rope(x, cos, sin):
    x1, x2 = jnp.split(x, 2, axis=-1)
    return jnp.concatenate([x1 * cos - x2 * sin, x2 * cos + x1 * sin], axis=-1)


def swiglu(x, w_gate, w_up, w_down):
    return (jax.nn.silu(x @ w_gate) * (x @ w_up)) @ w_down


def mla_attention(q_nope, q_rope, k_nope, k_rope, v):
    B, S, H, _ = q_nope.shape
    nb = S // Q_BLOCK
    scale = (MLA_NOPE + MLA_ROPE) ** -0.5
    key_pos = jnp.arange(S)

    def block(n):
        start = n * Q_BLOCK
        qn = lax.dynamic_slice_in_dim(q_nope, start, Q_BLOCK, axis=1)
        qr = lax.dynamic_slice_in_dim(q_rope, start, Q_BLOCK, axis=1)
        s = (jnp.einsum('bqhd,bkhd->bhqk', qn, k_nope)
             + jnp.einsum('bqhd,bkd->bhqk', qr, k_rope)).astype(jnp.float32) * scale
        q_pos = start + jnp.arange(Q_BLOCK)
        causal = key_pos[None, :] <= q_pos[:, None]
        s = jnp.where(causal[None, None], s, NEG)
        w = jax.nn.softmax(s, axis=-1).astype(v.dtype)
        return jnp.einsum('bhqk,bkhd->bqhd', w, v)

    out = lax.map(block, jnp.arange(nb))
    return out.transpose(1, 0, 2, 3, 4).reshape(B, S, H * v.shape[-1])


def dilated_branch(q, k, v, window, dilation):
    B, S, H, dh = q.shape
    L = S // dilation
    w_sub = window // dilation
    nb = -(-L // Q_BLOCK)
    Lp = nb * Q_BLOCK

    def to_sub(t):
        t = t.reshape(B, L, dilation, H, dh).transpose(0, 2, 1, 3, 4).reshape(B * dilation, L, H, dh)
        t = jnp.pad(t, ((0, 0), (0, Lp - L), (0, 0), (0, 0)))
        return t.reshape(B * dilation, nb, Q_BLOCK, H, dh)

    def with_prev(t):
        prev = jnp.pad(t, ((0, 0), (1, 0), (0, 0), (0, 0), (0, 0)))[:, :-1]
        return jnp.concatenate([prev, t], axis=2)

    qs, ks, vs = to_sub(q), to_sub(k), to_sub(v)
    kk, vv = with_prev(ks), with_prev(vs)
    s = jnp.einsum('gnqhd,gnkhd->gnhqk', qs, kk).astype(jnp.float32) * (dh ** -0.5)
    qi = jnp.arange(Q_BLOCK)[:, None]
    ki = jnp.arange(2 * Q_BLOCK)[None, :]
    dist = qi + Q_BLOCK - ki
    key_sub = jnp.arange(nb)[:, None, None] * Q_BLOCK + ki[None] - Q_BLOCK
    valid = (dist >= 0)[None] & (dist <= w_sub)[None] & (key_sub >= 0)
    s = jnp.where(valid[None, :, None], s, NEG)
    m = jnp.max(s, axis=-1, keepdims=True)
    e = jnp.exp(s - m)
    den = jnp.sum(e, axis=-1, keepdims=True)
    lse = (m + jnp.log(den))[..., 0]
    o = jnp.einsum('gnhqk,gnkhd->gnqhd', (e / den).astype(v.dtype), vv)
    o = (o.reshape(B * dilation, Lp, H, dh)[:, :L]
         .reshape(B, dilation, L, H, dh).transpose(0, 2, 1, 3, 4).reshape(B, S, H, dh))
    lse = (lse.transpose(0, 1, 3, 2).reshape(B * dilation, Lp, H)[:, :L]
           .reshape(B, dilation, L, H).transpose(0, 2, 1, 3).reshape(B, S, H))
    return o, lse


def dilated_attention(q, k, v):
    outs, lses = [], []
    for window, dilation in DIL_CONFIGS:
        o, l = dilated_branch(q, k, v, window, dilation)
        outs.append(o)
        lses.append(l)
    wts = jax.nn.softmax(jnp.stack(lses, axis=0), axis=0).astype(q.dtype)
    return jnp.einsum('kbsh,kbshd->bshd', wts, jnp.stack(outs, axis=0))


def hybrid_mixer(h, cos_m, sin_m, cos_d, sin_d, w_in, q_norm, w_q_up, kv_norm, w_kv_up, grp_gain, w_out):
    B, S, _ = h.shape
    z = h @ w_in
    c0 = Q_LORA
    c1 = c0 + KV_LORA
    c2 = c1 + MLA_ROPE
    q_lat, kv_lat, k_rope, qkv_d = jnp.split(z, [c0, c1, c2], axis=-1)
    q = (rmsnorm(q_lat, q_norm) @ w_q_up).reshape(B, S, MLA_HEADS, MLA_NOPE + MLA_ROPE)
    q_nope = q[..., :MLA_NOPE]
    q_rope = apply_rope(q[..., MLA_NOPE:], cos_m, sin_m)
    kv = (rmsnorm(kv_lat, kv_norm) @ w_kv_up).reshape(B, S, MLA_HEADS, MLA_NOPE + MLA_V)
    k_nope, v_mla = kv[..., :MLA_NOPE], kv[..., MLA_NOPE:]
    k_rope = apply_rope(k_rope[:, :, None, :], cos_m, sin_m)[:, :, 0]
    o_mla = mla_attention(q_nope, q_rope, k_nope, k_rope, v_mla)
    qd, kd, vd = [t.reshape(B, S, DIL_HEADS, DIL_HEAD_DIM) for t in jnp.split(qkv_d, 3, axis=-1)]
    qd = apply_rope(qd, cos_d, sin_d)
    kd = apply_rope(kd, cos_d, sin_d)
    o_dil = dilated_attention(qd, kd, vd).reshape(B, S, DIL_WIDTH)
    merged = jnp.concatenate([rmsnorm(o_mla, grp_gain[:MLA_WIDTH]),
                              rmsnorm(o_dil, grp_gain[MLA_WIDTH:])], axis=-1)
    return merged @ w_out


def setup_inputs(seed: int = 0) -> dict:
    key = jax.random.key(seed)
    ks = jax.random.split(key, 16)
    f32 = jnp.float32

    def w(k, shape, fan_in):
        return jax.random.normal(k, shape, f32) * (fan_in ** -0.5)

    def gain(k, shape):
        return 1.0 + 0.02 * jax.random.normal(k, shape, f32)

    return {
        "x": jax.random.normal(ks[0], (BATCH, SEQ, D_MODEL), f32),
        "p": jax.random.normal(ks[1], (DEPTH, BATCH, SEQ, PLE_DIM), f32),
        "positions": jnp.broadcast_to(jnp.arange(SEQ, dtype=jnp.int32), (BATCH, SEQ)),
        "norm_gains": gain(ks[2], (DEPTH, N_NORMS, D_MODEL)),
        "w_in": w(ks[3], (DEPTH, D_MODEL, N_IN), D_MODEL),
        "q_norm": gain(ks[4], (DEPTH, Q_LORA)),
        "w_q_up": w(ks[5], (DEPTH, Q_LORA, MLA_HEADS * (MLA_NOPE + MLA_ROPE)), Q_LORA),
        "kv_norm": gain(ks[6], (DEPTH, KV_LORA)),
        "w_kv_up": w(ks[7], (DEPTH, KV_LORA, MLA_HEADS * (MLA_NOPE + MLA_V)), KV_LORA),
        "group_out_norm": gain(ks[8], (DEPTH, D_MIX)),
        "w_out": w(ks[9], (DEPTH, D_MIX, D_MODEL), D_MIX),
        "ffn_gate": w(ks[10], (DEPTH, 2, D_MODEL, D_FF), D_MODEL),
        "ffn_up": w(ks[11], (DEPTH, 2, D_MODEL, D_FF), D_MODEL),
        "ffn_down": w(ks[12], (DEPTH, 2, D_FF, D_MODEL), D_FF),
        "w_ple": w(ks[13], (DEPTH, PLE_DIM, D_MODEL), PLE_DIM),
        "w_ple_gate": w(ks[14], (DEPTH, D_MODEL, D_MODEL), D_MODEL),
    }


def reference(x, p, positions, norm_gains, w_in, q_norm, w_q_up, kv_norm, w_kv_up,
              group_out_norm, w_out, ffn_gate, ffn_up, ffn_down, w_ple, w_ple_gate):
    cos_m, sin_m = rope_tables(positions, MLA_ROPE, x.dtype)
    cos_d, sin_d = rope_tables(positions, DIL_HEAD_DIM, x.dtype)
    h = x
    for i in range(DEPTH):
        g = norm_gains[i]
        f = swiglu(rmsnorm(h, g[0]), ffn_gate[i, 0], ffn_up[i, 0], ffn_down[i, 0])
        h = h + MACARON * rmsnorm(f, g[1])
        mix = hybrid_mixer(rmsnorm(h, g[2]), cos_m, sin_m, cos_d, sin_d, w_in[i], q_norm[i],
                           w_q_up[i], kv_norm[i], w_kv_up[i], group_out_norm[i], w_out[i])
        h = h + rmsnorm(mix, g[3])
        f = swiglu(rmsnorm(h, g[4]), ffn_gate[i, 1], ffn_up[i, 1], ffn_down[i, 1])
        h = h + MACARON * rmsnorm(f, g[5])
        gate = jax.nn.sigmoid(rmsnorm(h, g[6]) @ w_ple_gate[i])
        h = h + rmsnorm((p[i].astype(h.dtype) @ w_ple[i]) * gate, g[7])
    return h
```

```python
import functools

import jax
import jax.numpy as jnp
from jax import lax
from jax.experimental import pallas as pl
from jax.experimental.pallas import tpu as pltpu

F32 = jnp.float32
BF16 = jnp.bfloat16

D_MODEL = 1024
PLE_DIM = 256
MLA_HEADS = 8
MLA_NOPE = 64
MLA_ROPE = 32
MLA_V = 64
Q_LORA = 256
KV_LORA = 128
DIL_HEADS = 8
DIL_HEAD_DIM = 64
DIL_CONFIGS = ((128, 1), (512, 4), (2048, 16))
D_FF = 2816
MACARON = 0.5
ROPE_THETA = 10000.0
EPS = 1e-6
NEG = -1e30

LANES = 128
MLA_QK = MLA_NOPE + MLA_ROPE
MLA_WIDTH = MLA_HEADS * MLA_V
DIL_WIDTH = DIL_HEADS * DIL_HEAD_DIM
HEAD_PAIRS = DIL_HEADS // 2
N_IN_PAD = 2048
W_SUB = 128
DIL_CHUNK = 2048

TM = 512
MLA_T = 512
FF_CHUNKS = ((0, 512), (512, 1024), (1024, 1536), (1536, 2048), (2048, 2560), (2560, 2816))
VMEM_LIMIT = 56 * 1024 * 1024


def _rms(x, g):
    ms = jnp.mean(x * x, axis=-1, keepdims=True)
    return x * lax.rsqrt(ms + EPS) * g


def _dot(a, b):
    return jnp.dot(a, b, preferred_element_type=F32)


def _dot_nt(a, b):
    return lax.dot_general(a, b, (((1,), (1,)), ((), ())), preferred_element_type=F32)


def _const_spec(shape, index):
    return pl.BlockSpec(shape, lambda *_: index, pipeline_mode=pl.Buffered(1))


def _params(*sem):
    return pltpu.CompilerParams(dimension_semantics=sem, vmem_limit_bytes=VMEM_LIMIT)


def _ffn_body(gi_pre, gi_post, h_ref, gains_ref, wg_ref, wu_ref, wd_ref, o_ref):
    h = h_ref[...]
    xn = _rms(h, gains_ref[gi_pre:gi_pre + 1, :]).astype(BF16)
    acc = None
    for c0, c1 in FF_CHUNKS:
        g = _dot(xn, wg_ref[:, c0:c1])
        u = _dot(xn, wu_ref[:, c0:c1])
        a = (g * jax.nn.sigmoid(g) * u).astype(BF16)
        part = _dot(a, wd_ref[c0:c1, :])
        acc = part if acc is None else acc + part
    o_ref[...] = h + MACARON * _rms(acc, gains_ref[gi_post:gi_post + 1, :])


def _ffn(h, gains, wg, wu, wd, layer, which):
    n = h.shape[0]
    gi = 0 if which == 0 else 4
    return pl.pallas_call(
        functools.partial(_ffn_body, gi, gi + 1),
        grid=(n // TM,),
        in_specs=[
            pl.BlockSpec((TM, D_MODEL), lambda t: (t, 0)),
            _const_spec((None, 8, D_MODEL), (layer, 0, 0)),
            _const_spec((None, None, D_MODEL, D_FF), (layer, which, 0, 0)),
            _const_spec((None, None, D_MODEL, D_FF), (layer, which, 0, 0)),
            _const_spec((None, None, D_FF, D_MODEL), (layer, which, 0, 0)),
        ],
        out_specs=pl.BlockSpec((TM, D_MODEL), lambda t: (t, 0)),
        out_shape=jax.ShapeDtypeStruct(h.shape, F32),
        compiler_params=_params("parallel"),
        name=f"ffn_l{layer}_{which}",
    )(h, gains, wg, wu, wd)


def _rope(x, c, sa, sb, half):
    return x * c + pltpu.roll(x, LANES - half, 1) * sa + pltpu.roll(x, half, 1) * sb


def _inproj_body(h_ref, gains_ref, win_ref, qn_ref, wq_ref, kvn_ref, wkv_ref, tab_ref,
                 qm_ref, km_ref, vm_ref, qd_ref, kd_ref, vd_ref):
    h = h_ref[0]
    hn = _rms(h, gains_ref[2:3, :]).astype(BF16)
    z = _dot(hn, win_ref[...])
    cm, sam, sbm = tab_ref[0, 0], tab_ref[1, 0], tab_ref[2, 0]
    cd, sad, sbd = tab_ref[3, 0], tab_ref[4, 0], tab_ref[5, 0]
    hm = MLA_ROPE // 2
    hd = DIL_HEAD_DIM // 2

    qn = _rms(z[:, :Q_LORA], qn_ref[...]).astype(BF16)
    q = _dot(qn, wq_ref[...])
    for hh in range(MLA_HEADS):
        sl = slice(hh * LANES, (hh + 1) * LANES)
        qm_ref[0, :, sl] = _rope(q[:, sl], cm, sam, sbm, hm).astype(BF16)

    kvn = _rms(z[:, Q_LORA:Q_LORA + KV_LORA], kvn_ref[...]).astype(BF16)
    kv = _dot(kvn, wkv_ref[...])
    kr = _rope(z[:, 384:512], cm, sam, sbm, hm)
    for hh in range(MLA_HEADS):
        sl = slice(hh * LANES, (hh + 1) * LANES)
        km_ref[0, :, sl] = (kv[:, sl] + kr).astype(BF16)
    vm_ref[0] = kv[:, MLA_HEADS * LANES:].astype(BF16)

    for p in range(HEAD_PAIRS):
        sl = slice(p * LANES, (p + 1) * LANES)
        qd_ref[0, p] = _rope(z[:, 512:1024][:, sl], cd, sad, sbd, hd).astype(BF16)
        kd_ref[0, p] = _rope(z[:, 1024:1536][:, sl], cd, sad, sbd, hd).astype(BF16)
        vd_ref[0, p] = z[:, 1536:2048][:, sl].astype(BF16)


def _inproj(h3, gains, win, qnorm, wq, kvnorm, wkv, tab, layer):
    b, s, _ = h3.shape
    tok = lambda bi, t: (bi, t, 0)
    pair = lambda bi, t: (bi, 0, t, 0)
    return pl.pallas_call(
        _inproj_body,
        grid=(b, s // TM),
        in_specs=[
            pl.BlockSpec((1, TM, D_MODEL), tok),
            _const_spec((None, 8, D_MODEL), (layer, 0, 0)),
            _const_spec((None, D_MODEL, N_IN_PAD), (layer, 0, 0)),
            _const_spec((None, 1, Q_LORA), (layer, 0, 0)),
            _const_spec((None, Q_LORA, MLA_HEADS * LANES), (layer, 0, 0)),
            _const_spec((None, 1, KV_LORA), (layer, 0, 0)),
            _const_spec((None, KV_LORA, MLA_HEADS * LANES + MLA_WIDTH), (layer, 0, 0)),
            pl.BlockSpec((6, 1, TM, LANES), lambda bi, t: (0, bi, t, 0)),
        ],
        out_specs=[
            pl.BlockSpec((1, TM, MLA_HEADS * LANES), tok),
            pl.BlockSpec((1, TM, MLA_HEADS * LANES), tok),
            pl.BlockSpec((1, TM, MLA_WIDTH), tok),
            pl.BlockSpec((1, HEAD_PAIRS, TM, LANES), pair),
            pl.BlockSpec((1, HEAD_PAIRS, TM, LANES), pair),
            pl.BlockSpec((1, HEAD_PAIRS, TM, LANES), pair),
        ],
        out_shape=[
            jax.ShapeDtypeStruct((b, s, MLA_HEADS * LANES), BF16),
            jax.ShapeDtypeStruct((b, s, MLA_HEADS * LANES), BF16),
            jax.ShapeDtypeStruct((b, s, MLA_WIDTH), BF16),
            jax.ShapeDtypeStruct((b, HEAD_PAIRS, s, LANES), BF16),
            jax.ShapeDtypeStruct((b, HEAD_PAIRS, s, LANES), BF16),
            jax.ShapeDtypeStruct((b, HEAD_PAIRS, s, LANES), BF16),
        ],
        compiler_params=_params("parallel", "parallel"),
        name=f"inproj_l{layer}",
    )(h3, gains, win, qnorm, wq, kvnorm, wkv, tab)


def _mla_body(q_ref, k_ref, v_ref, o_ref):
    iq = pl.program_id(2)
    t = MLA_T
    scale = MLA_QK ** -0.5
    row = lax.broadcasted_iota(jnp.int32, (t, t), 0)
    col = lax.broadcasted_iota(jnp.int32, (t, t), 1)
    causal = col <= row
    qs = [q_ref[0, :, hh * LANES:(hh + 1) * LANES] for hh in range(2)]

    def step(j, carry, masked):
        ks = pl.multiple_of(j * t, t)
        v = v_ref[0, pl.ds(ks, t), :]
        out = []
        for hh in range(2):
            m, l, acc = carry[hh]
            k = k_ref[0, pl.ds(ks, t), hh * LANES:(hh + 1) * LANES]
            s = _dot_nt(qs[hh], k) * scale
            if masked:
                s = jnp.where(causal, s, NEG)
            m_new = jnp.maximum(m, jnp.max(s, axis=-1, keepdims=True))
            alpha = jnp.exp(m - m_new)
            p = jnp.exp(s - m_new)
            l = alpha * l + jnp.sum(p, axis=-1, keepdims=True)
            acc = alpha * acc + _dot(p.astype(BF16), v)
            out.append((m_new, l, acc))
        return tuple(out)

    init = tuple((jnp.full((t, 1), NEG, F32), jnp.zeros((t, 1), F32), jnp.zeros((t, LANES), F32))
                 for _ in range(2))
    carry = lax.fori_loop(0, iq, lambda j, c: step(j, c, False), init)
    (_, l0, a0), (_, l1, a1) = step(iq, carry, True)
    lane = lax.broadcasted_iota(jnp.int32, (t, LANES), 1)
    o_ref[0] = jnp.where(lane < MLA_V, a0 / l0, a1 / l1)


def _mla(qm, km, vm, layer):
    b, s, _ = qm.shape
    t = MLA_T
    return pl.pallas_call(
        _mla_body,
        grid=(b, MLA_HEADS // 2, s // t),
        in_specs=[
            pl.BlockSpec((1, t, 2 * LANES), lambda bi, hp, iq: (bi, iq, hp)),
            pl.BlockSpec((1, s, 2 * LANES), lambda bi, hp, iq: (bi, 0, hp)),
            pl.BlockSpec((1, s, LANES), lambda bi, hp, iq: (bi, 0, hp)),
        ],
        out_specs=pl.BlockSpec((1, t, LANES), lambda bi, hp, iq: (bi, iq, hp)),
        out_shape=jax.ShapeDtypeStruct((b, s, MLA_WIDTH), F32),
        compiler_params=_params("parallel", "parallel", "arbitrary"),
        name=f"mla_l{layer}",
    )(qm, km, vm)


def _dil_body(q1, q4, q16, k1, k4, k16, k1p, k4p, k16p, v1, v4, v16, v1p, v4p, v16p,
              o_ref, kk1, kk4, kk16, vv1, vv4, vv16, o_s1, o_s4, o_s16, l_s1, l_s4, l_s16):
    c = pl.program_id(2)
    w = W_SUB
    qi = lax.broadcasted_iota(jnp.int32, (w, 2 * w), 0)
    ki = lax.broadcasted_iota(jnp.int32, (w, 2 * w), 1)
    dist = qi + w - ki
    band = (dist >= 0) & (dist <= w)
    lane_q = lax.broadcasted_iota(jnp.int32, (w, LANES), 1)
    head_sel = [lane_q < DIL_HEAD_DIM, lane_q >= DIL_HEAD_DIM]
    qmul = [jnp.where(sel, DIL_HEAD_DIM ** -0.5, 0.0).astype(BF16) for sel in head_sel]
    o_scr = (o_s1, o_s4, o_s16)
    l_scr = (l_s1, l_s4, l_s16)

    branches = ((1, q1, k1, k1p, v1, v1p, kk1, vv1),
                (4, q4, k4, k4p, v4, v4p, kk4, vv4),
                (16, q16, k16, k16p, v16, v16p, kk16, vv16))
    for bi, (d, q_ref, k_ref, kp_ref, v_ref, vp_ref, kk, vv) in enumerate(branches):
        n_blk = DIL_CHUNK // (d * w)
        kk[0:w, :] = kp_ref[0, 0]
        kk[w:, :] = k_ref[0, 0]
        vv[0:w, :] = vp_ref[0, 0]
        vv[w:, :] = v_ref[0, 0]

        def block(n, r, d=d, q_ref=q_ref, kk=kk, vv=vv, bi=bi):
            ls = slice(r * LANES, (r + 1) * LANES)
            r0 = n * w if isinstance(n, int) else pl.multiple_of(n * w, w)
            qb = q_ref[0, 0, pl.ds(r0, w), ls]
            kb = kk[pl.ds(r0, 2 * w), ls]
            vb = vv[pl.ds(r0, 2 * w), ls]
            valid = band & (ki >= jnp.where((c == 0) & (n == 0), w, 0))
            outs, lses = [], []
            for hh in range(2):
                s = jnp.where(valid, _dot_nt(qb * qmul[hh], kb), NEG)
                m = jnp.max(s, axis=-1, keepdims=True)
                e = jnp.exp(s - m)
                den = jnp.sum(e, axis=-1, keepdims=True)
                lses.append(m + jnp.log(den))
                outs.append(_dot((e / den).astype(BF16), vb))
            o = jnp.where(head_sel[0], outs[0], outs[1])
            lse = jnp.where(head_sel[0], lses[0], lses[1])
            if d == 1:
                rows = pl.ds(r0, w)
            else:
                rows = pl.ds(n * (w * d) + r, w, stride=d)
            o_scr[bi][rows, :] = o
            l_scr[bi][rows, :] = lse

        for r in range(d):
            if n_blk == 1:
                block(0, r)
            else:
                def loop_body(n, carry, r=r, block=block):
                    block(n, r)
                    return carry
                lax.fori_loop(0, n_blk, loop_body, 0)

    l0, l1, l2 = l_s1[...], l_s4[...], l_s16[...]
    m = jnp.maximum(jnp.maximum(l0, l1), l2)
    e0, e1, e2 = jnp.exp(l0 - m), jnp.exp(l1 - m), jnp.exp(l2 - m)
    tot = e0 + e1 + e2
    o_ref[0] = (e0 / tot) * o_s1[...] + (e1 / tot) * o_s4[...] + (e2 / tot) * o_s16[...]


def _dilated(qd, kd, vd, layer):
    b, hp, s, _ = qd.shape
    w = W_SUB
    n_chunk = s // DIL_CHUNK
    views, cur_specs, prev_specs, scratch_k = [], [], [], []
    for _, d in DIL_CONFIGS:
        rows, width = DIL_CHUNK // d, d * LANES
        views.append((b, hp, s // d, width))
        cur_specs.append(pl.BlockSpec((1, 1, rows, width), lambda bi, p, c: (bi, p, c, 0)))
        per = rows // w
        prev_specs.append(pl.BlockSpec(
            (1, 1, w, width), lambda bi, p, c, per=per: (bi, p, jnp.maximum(c * per - 1, 0), 0)))
        scratch_k.append(pltpu.VMEM((rows + w, width), BF16))
    q_in = [qd.reshape(v) for v in views]
    k_in = [kd.reshape(v) for v in views]
    v_in = [vd.reshape(v) for v in views]
    return pl.pallas_call(
        _dil_body,
        grid=(b, hp, n_chunk),
        in_specs=cur_specs + cur_specs + prev_specs + cur_specs + prev_specs,
        out_specs=pl.BlockSpec((1, DIL_CHUNK, LANES), lambda bi, p, c: (bi, c, p)),
        out_shape=jax.ShapeDtypeStruct((b, s, DIL_WIDTH), F32),
        scratch_shapes=scratch_k + scratch_k + [pltpu.VMEM((DIL_CHUNK, LANES), F32)] * 6,
        compiler_params=_params("parallel", "parallel", "arbitrary"),
        name=f"dilated_l{layer}",
    )(*q_in, *k_in, *k_in, *v_in, *v_in)


def _outproj_body(h_ref, om_ref, od_ref, gains_ref, gg_ref, wo_ref, o_ref):
    a = _rms(om_ref[...], gg_ref[:, :MLA_WIDTH])
    b = _rms(od_ref[...], gg_ref[:, MLA_WIDTH:])
    merged = jnp.concatenate([a, b], axis=-1).astype(BF16)
    mix = _dot(merged, wo_ref[...])
    o_ref[...] = h_ref[...] + _rms(mix, gains_ref[3:4, :])


def _outproj(h, om, od, gains, gg, wo, layer):
    n = h.shape[0]
    return pl.pallas_call(
        _outproj_body,
        grid=(n // TM,),
        in_specs=[
            pl.BlockSpec((TM, D_MODEL), lambda t: (t, 0)),
            pl.BlockSpec((TM, MLA_WIDTH), lambda t: (t, 0)),
            pl.BlockSpec((TM, DIL_WIDTH), lambda t: (t, 0)),
            _const_spec((None, 8, D_MODEL), (layer, 0, 0)),
            _const_spec((None, 1, D_MODEL), (layer, 0, 0)),
            _const_spec((None, D_MODEL, D_MODEL), (layer, 0, 0)),
        ],
        out_specs=pl.BlockSpec((TM, D_MODEL), lambda t: (t, 0)),
        out_shape=jax.ShapeDtypeStruct(h.shape, F32),
        compiler_params=_params("parallel"),
        name=f"outproj_l{layer}",
    )(h, om, od, gains, gg, wo)


def _ple_body(h_ref, p_ref, gains_ref, wp_ref, wgate_ref, o_ref):
    h = h_ref[...]
    gate = jax.nn.sigmoid(_dot(_rms(h, gains_ref[6:7, :]).astype(BF16), wgate_ref[...]))
    e = _dot(p_ref[0].astype(BF16), wp_ref[...])
    o_ref[...] = h + _rms(e * gate, gains_ref[7:8, :])


def _ple(h, p3, gains, wp, wgate, layer):
    n = h.shape[0]
    return pl.pallas_call(
        _ple_body,
        grid=(n // TM,),
        in_specs=[
            pl.BlockSpec((TM, D_MODEL), lambda t: (t, 0)),
            pl.BlockSpec((1, TM, PLE_DIM), lambda t: (layer, t, 0)),
            _const_spec((None, 8, D_MODEL), (layer, 0, 0)),
            _const_spec((None, PLE_DIM, D_MODEL), (layer, 0, 0)),
            _const_spec((None, D_MODEL, D_MODEL), (layer, 0, 0)),
        ],
        out_specs=pl.BlockSpec((TM, D_MODEL), lambda t: (t, 0)),
        out_shape=jax.ShapeDtypeStruct(h.shape, F32),
        compiler_params=_params("parallel"),
        name=f"ple_l{layer}",
    )(h, p3, gains, wp, wgate)


def _prep_weights(w_in, w_q_up, w_kv_up):
    depth = w_in.shape[0]
    c1 = Q_LORA + KV_LORA
    c2 = c1 + MLA_ROPE
    zeros = lambda n: jnp.zeros((depth, D_MODEL, n), w_in.dtype)
    win = jnp.concatenate(
        [w_in[..., :c1], zeros(MLA_NOPE), w_in[..., c1:c2], zeros(LANES - MLA_QK), w_in[..., c2:]],
        axis=-1).astype(BF16)
    wq = w_q_up.reshape(depth, Q_LORA, MLA_HEADS, MLA_QK)
    wq = jnp.pad(wq, ((0, 0), (0, 0), (0, 0), (0, LANES - MLA_QK)))
    wq = wq.reshape(depth, Q_LORA, MLA_HEADS * LANES).astype(BF16)
    wkv = w_kv_up.reshape(depth, KV_LORA, MLA_HEADS, MLA_NOPE + MLA_V)
    wk = jnp.pad(wkv[..., :MLA_NOPE], ((0, 0), (0, 0), (0, 0), (0, LANES - MLA_NOPE)))
    wk = wk.reshape(depth, KV_LORA, MLA_HEADS * LANES)
    wv = wkv[..., MLA_NOPE:].reshape(depth, KV_LORA, MLA_WIDTH)
    wkv = jnp.concatenate([wk, wv], axis=-1).astype(BF16)
    return win, wq, wkv


def _rope_tables(positions):
    pos = positions.astype(F32)[..., None]
    shape = positions.shape

    def cos_sin(dim):
        inv = ROPE_THETA ** (-jnp.arange(0, dim, 2, dtype=F32) / dim)
        ang = pos * inv
        return jnp.cos(ang), jnp.sin(ang)

    one = lambda n: jnp.ones(shape + (n,), F32)
    zero = lambda n: jnp.zeros(shape + (n,), F32)
    cat = lambda xs: jnp.concatenate(xs, axis=-1)
    cm, sm = cos_sin(MLA_ROPE)
    cd, sd = cos_sin(DIL_HEAD_DIM)
    hm, hd = MLA_ROPE // 2, DIL_HEAD_DIM // 2
    pad = LANES - MLA_QK
    return jnp.stack([
        cat([one(MLA_NOPE), cm, cm, one(pad)]),
        cat([zero(MLA_NOPE), -sm, zero(hm), zero(pad)]),
        cat([zero(MLA_NOPE), zero(hm), sm, zero(pad)]),
        cat([cd, cd, cd, cd]),
        cat([-sd, zero(hd), -sd, zero(hd)]),
        cat([zero(hd), sd, zero(hd), sd]),
    ], axis=0)


def kernel(x, p, positions, norm_gains, w_in, q_norm, w_q_up, kv_norm, w_kv_up, group_out_norm, w_out,
           ffn_gate, ffn_up, ffn_down, w_ple, w_ple_gate):
    b, s, d = x.shape
    depth = norm_gains.shape[0]
    n = b * s
    assert s % DIL_CHUNK == 0 and s % MLA_T == 0 and s % TM == 0 and d == D_MODEL

    tab = _rope_tables(positions)
    win, wq, wkv = _prep_weights(w_in, w_q_up, w_kv_up)
    wo = w_out.astype(BF16)
    wg, wu, wd = ffn_gate.astype(BF16), ffn_up.astype(BF16), ffn_down.astype(BF16)
    wp, wpg = w_ple.astype(BF16), w_ple_gate.astype(BF16)
    p3 = p.reshape(depth, n, PLE_DIM)
    q_norm = q_norm.reshape(depth, 1, Q_LORA)
    kv_norm = kv_norm.reshape(depth, 1, KV_LORA)
    group_out_norm = group_out_norm.reshape(depth, 1, D_MODEL)

    h = x.reshape(n, d)
    for i in range(depth):
        h = _ffn(h, norm_gains, wg, wu, wd, i, 0)
        qm, km, vm, qd, kd, vd = _inproj(h.reshape(b, s, d), norm_gains, win, q_norm, wq, kv_norm, wkv, tab, i)
        om = _mla(qm, km, vm, i)
        od = _dilated(qd, kd, vd, i)
        h = _outproj(h, om.reshape(n, MLA_WIDTH), od.reshape(n, DIL_WIDTH), norm_gains, group_out_norm, wo, i)
        h = _ffn(h, norm_gains, wg, wu, wd, i, 1)
        h = _ple(h, p3, norm_gains, wp, wpg, i)
    return h.reshape(b, s, d)
```

```python
import functools

import jax
import jax.numpy as jnp
from jax import lax
from jax.experimental import pallas as pl
from jax.experimental.pallas import tpu as pltpu

F32 = jnp.float32
BF16 = jnp.bfloat16

D_MODEL = 1024
PLE_DIM = 256
MLA_HEADS = 8
MLA_NOPE = 64
MLA_ROPE = 32
MLA_V = 64
Q_LORA = 256
KV_LORA = 128
DIL_HEADS = 8
DIL_HEAD_DIM = 64
DIL_CONFIGS = ((128, 1), (512, 4), (2048, 16))
D_FF = 2816
MACARON = 0.5
ROPE_THETA = 10000.0
EPS = 1e-6
NEG = -1e30

LANES = 128
MLA_QK = MLA_NOPE + MLA_ROPE
MLA_WIDTH = MLA_HEADS * MLA_V
DIL_WIDTH = DIL_HEADS * DIL_HEAD_DIM
HEAD_PAIRS = DIL_HEADS // 2
N_IN_PAD = 2048
W_SUB = 128
DIL_CHUNK = 2048

TM = 512
MLA_T = 1024
MLA_KB = 1024
MLA_EXP2_SCALE = (MLA_QK ** -0.5) * 1.4426950408889634
MLA_QSUB = 256
MLA_KC = 128
MLA_LOOKAHEAD = 7
FF_CHUNKS = ((0, 512), (512, 1024), (1024, 1536), (1536, 2048), (2048, 2560), (2560, 2816))
VMEM_LIMIT = 56 * 1024 * 1024


def _rms(x, g):
    ms = jnp.mean(x * x, axis=-1, keepdims=True)
    return x * lax.rsqrt(ms + EPS) * g


def _dot(a, b):
    return jnp.dot(a, b, preferred_element_type=F32)


def _dot_nt(a, b):
    return lax.dot_general(a, b, (((1,), (1,)), ((), ())), preferred_element_type=F32)


def _const_spec(shape, index):
    return pl.BlockSpec(shape, lambda *_: index, pipeline_mode=pl.Buffered(1))


def _params(*sem):
    return pltpu.CompilerParams(dimension_semantics=sem, vmem_limit_bytes=VMEM_LIMIT)


def _ffn_body(gi_pre, gi_post, h_ref, gains_ref, wg_ref, wu_ref, wd_ref, o_ref):
    h = h_ref[...]
    xn = _rms(h, gains_ref[gi_pre:gi_pre + 1, :]).astype(BF16)
    acc = None
    for c0, c1 in FF_CHUNKS:
        g = _dot(xn, wg_ref[:, c0:c1])
        u = _dot(xn, wu_ref[:, c0:c1])
        a = (g * jax.nn.sigmoid(g) * u).astype(BF16)
        part = _dot(a, wd_ref[c0:c1, :])
        acc = part if acc is None else acc + part
    o_ref[...] = h + MACARON * _rms(acc, gains_ref[gi_post:gi_post + 1, :])


def _ffn(h, gains, wg, wu, wd, layer, which):
    n = h.shape[0]
    gi = 0 if which == 0 else 4
    return pl.pallas_call(
        functools.partial(_ffn_body, gi, gi + 1),
        grid=(n // TM,),
        in_specs=[
            pl.BlockSpec((TM, D_MODEL), lambda t: (t, 0)),
            _const_spec((None, 8, D_MODEL), (layer, 0, 0)),
            _const_spec((None, None, D_MODEL, D_FF), (layer, which, 0, 0)),
            _const_spec((None, None, D_MODEL, D_FF), (layer, which, 0, 0)),
            _const_spec((None, None, D_FF, D_MODEL), (layer, which, 0, 0)),
        ],
        out_specs=pl.BlockSpec((TM, D_MODEL), lambda t: (t, 0)),
        out_shape=jax.ShapeDtypeStruct(h.shape, F32),
        compiler_params=_params("parallel"),
        name=f"ffn_l{layer}_{which}",
    )(h, gains, wg, wu, wd)


def _rope(x, c, sa, sb, half):
    return x * c + pltpu.roll(x, LANES - half, 1) * sa + pltpu.roll(x, half, 1) * sb


def _inproj_body(h_ref, gains_ref, win_ref, qn_ref, wq_ref, kvn_ref, wkv_ref, tab_ref,
                 qt_ref, km_ref, vt_ref, *rest):
    n_br = len(DIL_CONFIGS)
    qd_refs, kd_refs, vd_refs = rest[:n_br], rest[n_br:2 * n_br], rest[2 * n_br:3 * n_br]
    stage_ref = rest[3 * n_br]
    h = h_ref[0]
    hn = _rms(h, gains_ref[2:3, :]).astype(BF16)
    z = _dot(hn, win_ref[...])
    cm, sam, sbm = tab_ref[0, 0], tab_ref[1, 0], tab_ref[2, 0]
    cd, sad, sbd = tab_ref[3, 0], tab_ref[4, 0], tab_ref[5, 0]
    hm = MLA_ROPE // 2
    hd = DIL_HEAD_DIM // 2

    qn = _rms(z[:, :Q_LORA], qn_ref[...]).astype(BF16)
    q = _dot(qn, wq_ref[...])
    for hh in range(MLA_HEADS):
        sl = slice(hh * LANES, (hh + 1) * LANES)
        qt_ref[0, sl, :] = (_rope(q[:, sl], cm, sam, sbm, hm) * MLA_EXP2_SCALE).T.astype(BF16)

    kvn = _rms(z[:, Q_LORA:Q_LORA + KV_LORA], kvn_ref[...]).astype(BF16)
    kv = _dot(kvn, wkv_ref[...])
    kr = _rope(z[:, 384:512], cm, sam, sbm, hm)
    for hh in range(MLA_HEADS):
        sl = slice(hh * LANES, (hh + 1) * LANES)
        km_ref[0, :, sl] = (kv[:, sl] + kr).astype(BF16)
    v = kv[:, MLA_HEADS * LANES:]
    for p in range(MLA_HEADS // 2):
        for tc in range(TM // MLA_KC):
            tile = v[tc * MLA_KC:(tc + 1) * MLA_KC, p * LANES:(p + 1) * LANES]
            vt_ref[0, p, tc] = tile.T.astype(BF16)

    for which, out_refs in enumerate((qd_refs, kd_refs, vd_refs)):
        for p in range(HEAD_PAIRS):
            x = z[:, 512 * (which + 1) + p * LANES:512 * (which + 1) + (p + 1) * LANES]
            if which < 2:
                x = _rope(x, cd, sad, sbd, hd)
            stage_ref[which, p] = x
            for (_, d), out_ref in zip(DIL_CONFIGS, out_refs):
                if d == 1:
                    out_ref[0, p] = x.astype(BF16)
                    continue
                for r in range(d):
                    rows = stage_ref[which, p, pl.ds(r, TM // d, stride=d), :]
                    out_ref[0, p, :, r * LANES:(r + 1) * LANES] = rows.astype(BF16)


def _inproj(h3, gains, win, qnorm, wq, kvnorm, wkv, tab, layer):
    b, s, _ = h3.shape
    tok = lambda bi, t: (bi, t, 0)
    pair = lambda bi, t: (bi, 0, t, 0)
    dil_specs = [pl.BlockSpec((1, HEAD_PAIRS, TM // d, d * LANES), pair) for _, d in DIL_CONFIGS]
    dil_shapes = [jax.ShapeDtypeStruct((b, HEAD_PAIRS, s // d, d * LANES), BF16) for _, d in DIL_CONFIGS]
    return pl.pallas_call(
        _inproj_body,
        grid=(b, s // TM),
        in_specs=[
            pl.BlockSpec((1, TM, D_MODEL), tok),
            _const_spec((None, 8, D_MODEL), (layer, 0, 0)),
            _const_spec((None, D_MODEL, N_IN_PAD), (layer, 0, 0)),
            _const_spec((None, 1, Q_LORA), (layer, 0, 0)),
            _const_spec((None, Q_LORA, MLA_HEADS * LANES), (layer, 0, 0)),
            _const_spec((None, 1, KV_LORA), (layer, 0, 0)),
            _const_spec((None, KV_LORA, MLA_HEADS * LANES + MLA_WIDTH), (layer, 0, 0)),
            pl.BlockSpec((6, 1, TM, LANES), lambda bi, t: (0, bi, t, 0)),
        ],
        out_specs=[
            pl.BlockSpec((1, MLA_HEADS * LANES, TM), lambda bi, t: (bi, 0, t)),
            pl.BlockSpec((1, TM, MLA_HEADS * LANES), tok),
            pl.BlockSpec((1, MLA_HEADS // 2, TM // MLA_KC, 2 * MLA_V, MLA_KC), lambda bi, t: (bi, 0, t, 0, 0)),
        ] + dil_specs * 3,
        out_shape=[
            jax.ShapeDtypeStruct((b, MLA_HEADS * LANES, s), BF16),
            jax.ShapeDtypeStruct((b, s, MLA_HEADS * LANES), BF16),
            jax.ShapeDtypeStruct((b, MLA_HEADS // 2, s // MLA_KC, 2 * MLA_V, MLA_KC), BF16),
        ] + dil_shapes * 3,
        scratch_shapes=[pltpu.VMEM((3, HEAD_PAIRS, TM, LANES), F32)],
        compiler_params=_params("parallel", "parallel"),
        name=f"inproj_l{layer}",
    )(h3, gains, win, qnorm, wq, kvnorm, wkv, tab)


def _mla_body(qt_ref, k_ref, vt_ref, o_ref):
    iq = pl.program_id(2)
    t, qs, kc = MLA_T, MLA_QSUB, MLA_KC
    n_sub = t // qs
    n_chunk = t // kc
    key_i = lax.broadcasted_iota(jnp.int32, (kc, qs), 0)
    qry_i = lax.broadcasted_iota(jnp.int32, (kc, qs), 1)
    streams = [(hh, u) for hh in range(2) for u in range(n_sub)]
    qts = {(hh, u): qt_ref[0, hh * LANES:(hh + 1) * LANES, u * qs:(u + 1) * qs] for hh, u in streams}

    def scores(si, jc, key_off):
        hh, u = streams[si]
        k = k_ref[0, pl.ds(pl.multiple_of(jc * kc, kc), kc), hh * LANES:(hh + 1) * LANES]
        s = _dot(k, qts[(hh, u)])
        if key_off is not None:
            s = jnp.where(key_i + (key_off - u * qs) <= qry_i, s, NEG)
        return s

    def update(state, s, si, jc):
        m, l, acc = state
        hh, _ = streams[si]
        m_new = jnp.maximum(m, jnp.max(s, axis=0, keepdims=True))
        alpha = jnp.exp2(m - m_new)
        p = jnp.exp2(s - m_new)
        l = alpha * l + jnp.sum(p, axis=0, keepdims=True)
        vt = vt_ref[0, 0, jc, hh * MLA_V:(hh + 1) * MLA_V, :]
        acc = alpha * acc + _dot(vt, p.astype(BF16))
        return m_new, l, acc

    def run(units, state):
        state = list(state)
        pending = {}
        for n in range(len(units) + MLA_LOOKAHEAD):
            if n < len(units):
                pending[n] = scores(*units[n])
            if n >= MLA_LOOKAHEAD:
                si, jc, _ = units[n - MLA_LOOKAHEAD]
                state[si] = update(state[si], pending.pop(n - MLA_LOOKAHEAD), si, jc)
        return tuple(state)

    per_iter = MLA_KB // kc

    def full_block(j, carry):
        return run([(si, j * per_iter + i, None) for i in range(per_iter) for si in range(len(streams))], carry)

    init = tuple((jnp.full((1, qs), NEG, F32), jnp.zeros((1, qs), F32), jnp.zeros((MLA_V, qs), F32))
                 for _ in streams)
    state = lax.fori_loop(0, iq * (t // MLA_KB), full_block, init)
    diag = []
    for i in range(n_chunk):
        for si, (hh, u) in enumerate(streams):
            if i * kc < (u + 1) * qs:
                fully_visible = (i + 1) * kc <= u * qs
                diag.append((si, iq * n_chunk + i, None if fully_visible else i * kc))
    state = run(diag, state)
    for u in range(n_sub):
        per_head = [state[streams.index((hh, u))] for hh in range(2)]
        o_t = jnp.concatenate([acc / l for _, l, acc in per_head], axis=0)
        o_ref[0, u * qs:(u + 1) * qs, :] = o_t.T


def _mla(qt, km, vtb, layer):
    b, _, s = qt.shape
    t = MLA_T
    return pl.pallas_call(
        _mla_body,
        grid=(b, MLA_HEADS // 2, s // t),
        in_specs=[
            pl.BlockSpec((1, 2 * LANES, t), lambda bi, hp, iq: (bi, hp, iq)),
            pl.BlockSpec((1, s, 2 * LANES), lambda bi, hp, iq: (bi, 0, hp)),
            pl.BlockSpec((1, 1, s // MLA_KC, 2 * MLA_V, MLA_KC), lambda bi, hp, iq: (bi, hp, 0, 0, 0)),
        ],
        out_specs=pl.BlockSpec((1, t, LANES), lambda bi, hp, iq: (bi, iq, hp)),
        out_shape=jax.ShapeDtypeStruct((b, s, MLA_WIDTH), F32),
        compiler_params=_params("parallel", "parallel", "arbitrary"),
        name=f"mla_l{layer}",
    )(qt, km, vtb)


def _dil_body(q1, q4, q16, k1, k4, k16, k1p, k4p, k16p, v1, v4, v16, v1p, v4p, v16p,
              o_ref, kk1, kk4, kk16, vv1, vv4, vv16, o_s1, o_s4, o_s16, l_s1, l_s4, l_s16):
    c = pl.program_id(2)
    w = W_SUB
    qi = lax.broadcasted_iota(jnp.int32, (w, 2 * w), 0)
    ki = lax.broadcasted_iota(jnp.int32, (w, 2 * w), 1)
    dist = qi + w - ki
    band = (dist >= 0) & (dist <= w)
    lane_q = lax.broadcasted_iota(jnp.int32, (w, LANES), 1)
    head_sel = [lane_q < DIL_HEAD_DIM, lane_q >= DIL_HEAD_DIM]
    qmul = [jnp.where(sel, DIL_HEAD_DIM ** -0.5, 0.0).astype(BF16) for sel in head_sel]
    o_scr = (o_s1, o_s4, o_s16)
    l_scr = (l_s1, l_s4, l_s16)

    branches = ((1, q1, k1, k1p, v1, v1p, kk1, vv1),
                (4, q4, k4, k4p, v4, v4p, kk4, vv4),
                (16, q16, k16, k16p, v16, v16p, kk16, vv16))
    for bi, (d, q_ref, k_ref, kp_ref, v_ref, vp_ref, kk, vv) in enumerate(branches):
        n_blk = DIL_CHUNK // (d * w)
        kk[0:w, :] = kp_ref[0, 0]
        kk[w:, :] = k_ref[0, 0]
        vv[0:w, :] = vp_ref[0, 0]
        vv[w:, :] = v_ref[0, 0]

        def block(n, r, d=d, q_ref=q_ref, kk=kk, vv=vv, bi=bi):
            ls = slice(r * LANES, (r + 1) * LANES)
            r0 = n * w if isinstance(n, int) else pl.multiple_of(n * w, w)
            qb = q_ref[0, 0, pl.ds(r0, w), ls]
            kb = kk[pl.ds(r0, 2 * w), ls]
            vb = vv[pl.ds(r0, 2 * w), ls]
            valid = band & (ki >= jnp.where((c == 0) & (n == 0), w, 0))
            outs, lses = [], []
            for hh in range(2):
                s = jnp.where(valid, _dot_nt(qb * qmul[hh], kb), NEG)
                m = jnp.max(s, axis=-1, keepdims=True)
                e = jnp.exp(s - m)
                den = jnp.sum(e, axis=-1, keepdims=True)
                lses.append(m + jnp.log(den))
                outs.append(_dot((e / den).astype(BF16), vb))
            o = jnp.where(head_sel[0], outs[0], outs[1])
            lse = jnp.where(head_sel[0], lses[0], lses[1])
            if d == 1:
                rows = pl.ds(r0, w)
            else:
                rows = pl.ds(n * (w * d) + r, w, stride=d)
            o_scr[bi][rows, :] = o
            l_scr[bi][rows, :] = lse

        for r in range(d):
            for n in range(n_blk):
                block(n, r)

    l0, l1, l2 = l_s1[...], l_s4[...], l_s16[...]
    m = jnp.maximum(jnp.maximum(l0, l1), l2)
    e0, e1, e2 = jnp.exp(l0 - m), jnp.exp(l1 - m), jnp.exp(l2 - m)
    tot = e0 + e1 + e2
    o_ref[0] = (e0 / tot) * o_s1[...] + (e1 / tot) * o_s4[...] + (e2 / tot) * o_s16[...]


def _dilated(q_in, k_in, v_in, layer):
    b, hp, s, _ = q_in[0].shape
    w = W_SUB
    n_chunk = s // DIL_CHUNK
    cur_specs, prev_specs, scratch_k = [], [], []
    for _, d in DIL_CONFIGS:
        rows, width = DIL_CHUNK // d, d * LANES
        cur_specs.append(pl.BlockSpec((1, 1, rows, width), lambda bi, p, c: (bi, p, c, 0)))
        per = rows // w
        prev_specs.append(pl.BlockSpec(
            (1, 1, w, width), lambda bi, p, c, per=per: (bi, p, jnp.maximum(c * per - 1, 0), 0)))
        scratch_k.append(pltpu.VMEM((rows + w, width), BF16))
    return pl.pallas_call(
        _dil_body,
        grid=(b, hp, n_chunk),
        in_specs=cur_specs + cur_specs + prev_specs + cur_specs + prev_specs,
        out_specs=pl.BlockSpec((1, DIL_CHUNK, LANES), lambda bi, p, c: (bi, c, p)),
        out_shape=jax.ShapeDtypeStruct((b, s, DIL_WIDTH), F32),
        scratch_shapes=scratch_k + scratch_k + [pltpu.VMEM((DIL_CHUNK, LANES), F32)] * 6,
        compiler_params=_params("parallel", "parallel", "arbitrary"),
        name=f"dilated_l{layer}",
    )(*q_in, *k_in, *k_in, *v_in, *v_in)


def _outproj_body(h_ref, om_ref, od_ref, gains_ref, gg_ref, wo_ref, o_ref):
    a = _rms(om_ref[...], gg_ref[:, :MLA_WIDTH])
    b = _rms(od_ref[...], gg_ref[:, MLA_WIDTH:])
    merged = jnp.concatenate([a, b], axis=-1).astype(BF16)
    mix = _dot(merged, wo_ref[...])
    o_ref[...] = h_ref[...] + _rms(mix, gains_ref[3:4, :])


def _outproj(h, om, od, gains, gg, wo, layer):
    n = h.shape[0]
    return pl.pallas_call(
        _outproj_body,
        grid=(n // TM,),
        in_specs=[
            pl.BlockSpec((TM, D_MODEL), lambda t: (t, 0)),
            pl.BlockSpec((TM, MLA_WIDTH), lambda t: (t, 0)),
            pl.BlockSpec((TM, DIL_WIDTH), lambda t: (t, 0)),
            _const_spec((None, 8, D_MODEL), (layer, 0, 0)),
            _const_spec((None, 1, D_MODEL), (layer, 0, 0)),
            _const_spec((None, D_MODEL, D_MODEL), (layer, 0, 0)),
        ],
        out_specs=pl.BlockSpec((TM, D_MODEL), lambda t: (t, 0)),
        out_shape=jax.ShapeDtypeStruct(h.shape, F32),
        compiler_params=_params("parallel"),
        name=f"outproj_l{layer}",
    )(h, om, od, gains, gg, wo)


def _ple_body(h_ref, p_ref, gains_ref, wp_ref, wgate_ref, o_ref):
    h = h_ref[...]
    gate = jax.nn.sigmoid(_dot(_rms(h, gains_ref[6:7, :]).astype(BF16), wgate_ref[...]))
    e = _dot(p_ref[0].astype(BF16), wp_ref[...])
    o_ref[...] = h + _rms(e * gate, gains_ref[7:8, :])


def _ple(h, p3, gains, wp, wgate, layer):
    n = h.shape[0]
    return pl.pallas_call(
        _ple_body,
        grid=(n // TM,),
        in_specs=[
            pl.BlockSpec((TM, D_MODEL), lambda t: (t, 0)),
            pl.BlockSpec((1, TM, PLE_DIM), lambda t: (layer, t, 0)),
            _const_spec((None, 8, D_MODEL), (layer, 0, 0)),
            _const_spec((None, PLE_DIM, D_MODEL), (layer, 0, 0)),
            _const_spec((None, D_MODEL, D_MODEL), (layer, 0, 0)),
        ],
        out_specs=pl.BlockSpec((TM, D_MODEL), lambda t: (t, 0)),
        out_shape=jax.ShapeDtypeStruct(h.shape, F32),
        compiler_params=_params("parallel"),
        name=f"ple_l{layer}",
    )(h, p3, gains, wp, wgate)


def _prep_weights(w_in, w_q_up, w_kv_up):
    depth = w_in.shape[0]
    c1 = Q_LORA + KV_LORA
    c2 = c1 + MLA_ROPE
    zeros = lambda n: jnp.zeros((depth, D_MODEL, n), w_in.dtype)
    win = jnp.concatenate(
        [w_in[..., :c1], zeros(MLA_NOPE), w_in[..., c1:c2], zeros(LANES - MLA_QK), w_in[..., c2:]],
        axis=-1).astype(BF16)
    wq = w_q_up.reshape(depth, Q_LORA, MLA_HEADS, MLA_QK)
    wq = jnp.pad(wq, ((0, 0), (0, 0), (0, 0), (0, LANES - MLA_QK)))
    wq = wq.reshape(depth, Q_LORA, MLA_HEADS * LANES).astype(BF16)
    wkv = w_kv_up.reshape(depth, KV_LORA, MLA_HEADS, MLA_NOPE + MLA_V)
    wk = jnp.pad(wkv[..., :MLA_NOPE], ((0, 0), (0, 0), (0, 0), (0, LANES - MLA_NOPE)))
    wk = wk.reshape(depth, KV_LORA, MLA_HEADS * LANES)
    wv = wkv[..., MLA_NOPE:].reshape(depth, KV_LORA, MLA_WIDTH)
    wkv = jnp.concatenate([wk, wv], axis=-1).astype(BF16)
    return win, wq, wkv


def _rope_tables(positions):
    pos = positions.astype(F32)[..., None]
    shape = positions.shape

    def cos_sin(dim):
        inv = ROPE_THETA ** (-jnp.arange(0, dim, 2, dtype=F32) / dim)
        ang = pos * inv
        return jnp.cos(ang), jnp.sin(ang)

    one = lambda n: jnp.ones(shape + (n,), F32)
    zero = lambda n: jnp.zeros(shape + (n,), F32)
    cat = lambda xs: jnp.concatenate(xs, axis=-1)
    cm, sm = cos_sin(MLA_ROPE)
    cd, sd = cos_sin(DIL_HEAD_DIM)
    hm, hd = MLA_ROPE // 2, DIL_HEAD_DIM // 2
    pad = LANES - MLA_QK
    return jnp.stack([
        cat([one(MLA_NOPE), cm, cm, one(pad)]),
        cat([zero(MLA_NOPE), -sm, zero(hm), zero(pad)]),
        cat([zero(MLA_NOPE), zero(hm), sm, zero(pad)]),
        cat([cd, cd, cd, cd]),
        cat([-sd, zero(hd), -sd, zero(hd)]),
        cat([zero(hd), sd, zero(hd), sd]),
    ], axis=0)


def kernel(x, p, positions, norm_gains, w_in, q_norm, w_q_up, kv_norm, w_kv_up, group_out_norm, w_out,
           ffn_gate, ffn_up, ffn_down, w_ple, w_ple_gate):
    b, s, d = x.shape
    depth = norm_gains.shape[0]
    n = b * s
    assert s % DIL_CHUNK == 0 and s % MLA_T == 0 and s % TM == 0 and d == D_MODEL

    tab = _rope_tables(positions)
    win, wq, wkv = _prep_weights(w_in, w_q_up, w_kv_up)
    wo = w_out.astype(BF16)
    wg, wu, wd = ffn_gate.astype(BF16), ffn_up.astype(BF16), ffn_down.astype(BF16)
    wp, wpg = w_ple.astype(BF16), w_ple_gate.astype(BF16)
    p3 = p.reshape(depth, n, PLE_DIM)
    q_norm = q_norm.reshape(depth, 1, Q_LORA)
    kv_norm = kv_norm.reshape(depth, 1, KV_LORA)
    group_out_norm = group_out_norm.reshape(depth, 1, D_MODEL)

    h = x.reshape(n, d)
    for i in range(depth):
        h = _ffn(h, norm_gains, wg, wu, wd, i, 0)
        proj = _inproj(h.reshape(b, s, d), norm_gains, win, q_norm, wq, kv_norm, wkv, tab, i)
        om = _mla(*proj[:3], i)
        od = _dilated(proj[3:6], proj[6:9], proj[9:12], i)
        h = _outproj(h, om.reshape(n, MLA_WIDTH), od.reshape(n, DIL_WIDTH), norm_gains, group_out_norm, wo, i)
        h = _ffn(h, norm_gains, wg, wu, wd, i, 1)
        h = _ple(h, p3, norm_gains, wp, wpg, i)
    return h.reshape(b, s, d)
```

```python
import functools

import jax
import jax.numpy as jnp
from jax import lax
from jax.experimental import pallas as pl
from jax.experimental.pallas import tpu as pltpu

F32 = jnp.float32
BF16 = jnp.bfloat16

D_MODEL = 1024
PLE_DIM = 256
MLA_HEADS = 8
MLA_NOPE = 64
MLA_ROPE = 32
MLA_V = 64
Q_LORA = 256
KV_LORA = 128
DIL_HEADS = 8
DIL_HEAD_DIM = 64
DIL_CONFIGS = ((128, 1), (512, 4), (2048, 16))
D_FF = 2816
MACARON = 0.5
ROPE_THETA = 10000.0
EPS = 1e-6
NEG = -1e30

LANES = 128
MLA_QK = MLA_NOPE + MLA_ROPE
MLA_WIDTH = MLA_HEADS * MLA_V
DIL_WIDTH = DIL_HEADS * DIL_HEAD_DIM
HEAD_PAIRS = DIL_HEADS // 2
N_IN_PAD = 2048
W_SUB = 128
DIL_CHUNK = 2048

TM = 512
MLA_T = 1024
MLA_KB = 1024
MLA_EXP2_SCALE = (MLA_QK ** -0.5) * 1.4426950408889634
MLA_QSUB = 256
MLA_KC = 256
MLA_SUM_ROWS = 16
MLA_LOOKAHEAD = 6
FF_CHUNKS = ((0, 512), (512, 1024), (1024, 1536), (1536, 2048), (2048, 2560), (2560, 2816))
VMEM_LIMIT = 56 * 1024 * 1024


def _rms(x, g):
    ms = jnp.mean(x * x, axis=-1, keepdims=True)
    return x * lax.rsqrt(ms + EPS) * g


def _dot(a, b):
    return jnp.dot(a, b, preferred_element_type=F32)


def _dot_nt(a, b):
    return lax.dot_general(a, b, (((1,), (1,)), ((), ())), preferred_element_type=F32)


def _const_spec(shape, index):
    return pl.BlockSpec(shape, lambda *_: index, pipeline_mode=pl.Buffered(1))


def _params(*sem):
    return pltpu.CompilerParams(dimension_semantics=sem, vmem_limit_bytes=VMEM_LIMIT)


def _ffn_body(gi_pre, gi_post, h_ref, gains_ref, wg_ref, wu_ref, wd_ref, o_ref):
    h = h_ref[...]
    xn = _rms(h, gains_ref[gi_pre:gi_pre + 1, :]).astype(BF16)
    acc = None
    for c0, c1 in FF_CHUNKS:
        g = _dot(xn, wg_ref[:, c0:c1])
        u = _dot(xn, wu_ref[:, c0:c1])
        a = (g * jax.nn.sigmoid(g) * u).astype(BF16)
        part = _dot(a, wd_ref[c0:c1, :])
        acc = part if acc is None else acc + part
    o_ref[...] = h + MACARON * _rms(acc, gains_ref[gi_post:gi_post + 1, :])


def _ffn(h, gains, wg, wu, wd, layer, which):
    n = h.shape[0]
    gi = 0 if which == 0 else 4
    return pl.pallas_call(
        functools.partial(_ffn_body, gi, gi + 1),
        grid=(n // TM,),
        in_specs=[
            pl.BlockSpec((TM, D_MODEL), lambda t: (t, 0)),
            _const_spec((None, 8, D_MODEL), (layer, 0, 0)),
            _const_spec((None, None, D_MODEL, D_FF), (layer, which, 0, 0)),
            _const_spec((None, None, D_MODEL, D_FF), (layer, which, 0, 0)),
            _const_spec((None, None, D_FF, D_MODEL), (layer, which, 0, 0)),
        ],
        out_specs=pl.BlockSpec((TM, D_MODEL), lambda t: (t, 0)),
        out_shape=jax.ShapeDtypeStruct(h.shape, F32),
        compiler_params=_params("parallel"),
        name=f"ffn_l{layer}_{which}",
    )(h, gains, wg, wu, wd)


def _rope(x, c, sa, sb, half):
    return x * c + pltpu.roll(x, LANES - half, 1) * sa + pltpu.roll(x, half, 1) * sb


def _inproj_body(h_ref, gains_ref, win_ref, qn_ref, wq_ref, kvn_ref, wkv_ref, tab_ref,
                 qt_ref, km_ref, vt_ref, *rest):
    n_br = len(DIL_CONFIGS)
    qd_refs, kd_refs, vd_refs = rest[:n_br], rest[n_br:2 * n_br], rest[2 * n_br:3 * n_br]
    stage_ref, stage4_ref = rest[3 * n_br:]
    assert [d for _, d in DIL_CONFIGS] == [1, 4, 16]
    h = h_ref[0]
    hn = _rms(h, gains_ref[2:3, :]).astype(BF16)
    z = _dot(hn, win_ref[...])
    cm, sam, sbm = tab_ref[0, 0], tab_ref[1, 0], tab_ref[2, 0]
    cd, sad, sbd = tab_ref[3, 0], tab_ref[4, 0], tab_ref[5, 0]
    hm = MLA_ROPE // 2
    hd = DIL_HEAD_DIM // 2

    qn = _rms(z[:, :Q_LORA], qn_ref[...]).astype(BF16)
    q = _dot(qn, wq_ref[...])
    n_nope = MLA_HEADS * MLA_NOPE
    nope_t = (q[:, :n_nope] * MLA_EXP2_SCALE).T
    rope_t = jnp.concatenate(
        [(_rope(q[:, n_nope + g * LANES:n_nope + (g + 1) * LANES], cm, sam, sbm, hm) * MLA_EXP2_SCALE).T
         for g in range(MLA_HEADS * MLA_ROPE // LANES)], axis=0)
    for hh in range(MLA_HEADS):
        r0 = hh * LANES
        qt_ref[0, r0:r0 + MLA_NOPE, :] = nope_t[hh * MLA_NOPE:(hh + 1) * MLA_NOPE].astype(BF16)
        qt_ref[0, r0 + MLA_NOPE:r0 + MLA_QK, :] = rope_t[hh * MLA_ROPE:(hh + 1) * MLA_ROPE].astype(BF16)
        qt_ref[0, r0 + MLA_QK:r0 + LANES, :] = jnp.zeros((LANES - MLA_QK, TM), BF16)

    kvn = _rms(z[:, Q_LORA:Q_LORA + KV_LORA], kvn_ref[...]).astype(BF16)
    kv = _dot(kvn, wkv_ref[...])
    kr = pltpu.roll(_rope(z[:, 384:512], cm, sam, sbm, hm), MLA_NOPE, 1)
    for hh in range(MLA_HEADS):
        sl = slice(hh * LANES, (hh + 1) * LANES)
        km_ref[0, :, sl] = (kv[:, sl] + kr).astype(BF16)
    v = kv[:, MLA_HEADS * LANES:]
    for p in range(MLA_HEADS // 2):
        for tc in range(TM // MLA_KC):
            tile = v[tc * MLA_KC:(tc + 1) * MLA_KC, p * LANES:(p + 1) * LANES]
            vt_ref[0, p, tc] = tile.T.astype(BF16)

    for which, out_refs in enumerate((qd_refs, kd_refs, vd_refs)):
        for p in range(HEAD_PAIRS):
            x = z[:, 512 * (which + 1) + p * LANES:512 * (which + 1) + (p + 1) * LANES]
            if which < 2:
                x = _rope(x, cd, sad, sbd, hd)
            stage_ref[which, p] = x
            out_ref1, out_ref4, out_ref16 = out_refs
            out_ref1[0, p] = x.astype(BF16)
            for r in range(4):
                rows = stage_ref[which, p, pl.ds(r, TM // 4, stride=4), :]
                out_ref4[0, p, :, r * LANES:(r + 1) * LANES] = rows.astype(BF16)
                stage4_ref[which, p, r] = rows
            for r in range(4):
                for j in range(4):
                    rows = stage4_ref[which, p, r, pl.ds(j, TM // 16, stride=4), :]
                    out_ref16[0, p, :, (4 * j + r) * LANES:(4 * j + r + 1) * LANES] = rows.astype(BF16)


def _inproj(h3, gains, win, qnorm, wq, kvnorm, wkv, tab, layer):
    b, s, _ = h3.shape
    tok = lambda bi, t: (bi, t, 0)
    pair = lambda bi, t: (bi, 0, t, 0)
    dil_specs = [pl.BlockSpec((1, HEAD_PAIRS, TM // d, d * LANES), pair) for _, d in DIL_CONFIGS]
    dil_shapes = [jax.ShapeDtypeStruct((b, HEAD_PAIRS, s // d, d * LANES), BF16) for _, d in DIL_CONFIGS]
    return pl.pallas_call(
        _inproj_body,
        grid=(b, s // TM),
        in_specs=[
            pl.BlockSpec((1, TM, D_MODEL), tok),
            _const_spec((None, 8, D_MODEL), (layer, 0, 0)),
            _const_spec((None, D_MODEL, N_IN_PAD), (layer, 0, 0)),
            _const_spec((None, 1, Q_LORA), (layer, 0, 0)),
            _const_spec((None, Q_LORA, MLA_HEADS * MLA_QK), (layer, 0, 0)),
            _const_spec((None, 1, KV_LORA), (layer, 0, 0)),
            _const_spec((None, KV_LORA, MLA_HEADS * LANES + MLA_WIDTH), (layer, 0, 0)),
            pl.BlockSpec((6, 1, TM, LANES), lambda bi, t: (0, bi, t, 0)),
        ],
        out_specs=[
            pl.BlockSpec((1, MLA_HEADS * LANES, TM), lambda bi, t: (bi, 0, t)),
            pl.BlockSpec((1, TM, MLA_HEADS * LANES), tok),
            pl.BlockSpec((1, MLA_HEADS // 2, TM // MLA_KC, 2 * MLA_V, MLA_KC), lambda bi, t: (bi, 0, t, 0, 0)),
        ] + dil_specs * 3,
        out_shape=[
            jax.ShapeDtypeStruct((b, MLA_HEADS * LANES, s), BF16),
            jax.ShapeDtypeStruct((b, s, MLA_HEADS * LANES), BF16),
            jax.ShapeDtypeStruct((b, MLA_HEADS // 2, s // MLA_KC, 2 * MLA_V, MLA_KC), BF16),
        ] + dil_shapes * 3,
        scratch_shapes=[pltpu.VMEM((3, HEAD_PAIRS, TM, LANES), F32),
                        pltpu.VMEM((3, HEAD_PAIRS, 4, TM // 4, LANES), F32)],
        compiler_params=_params("parallel", "parallel"),
        name=f"inproj_l{layer}",
    )(h3, gains, win, qnorm, wq, kvnorm, wkv, tab)


def _mla_body(qt_ref, k_ref, vt_ref, o_ref):
    iq = pl.program_id(2)
    t, qs, kc = MLA_T, MLA_QSUB, MLA_KC
    n_sub = t // qs
    n_chunk = t // kc
    key_i = lax.broadcasted_iota(jnp.int32, (kc, qs), 0)
    qry_i = lax.broadcasted_iota(jnp.int32, (kc, qs), 1)
    streams = [(hh, u) for hh in range(2) for u in range(n_sub)]
    qts = {(hh, u): qt_ref[0, hh * LANES:(hh + 1) * LANES, u * qs:(u + 1) * qs] for hh, u in streams}

    def scores(si, jc, key_off):
        hh, u = streams[si]
        k = k_ref[0, pl.ds(pl.multiple_of(jc * kc, kc), kc), hh * LANES:(hh + 1) * LANES]
        s = _dot(k, qts[(hh, u)])
        if key_off is not None:
            s = jnp.where(key_i + (key_off - u * qs) <= qry_i, s, NEG)
        return s

    ones_rows = (lax.broadcasted_iota(jnp.int32, (MLA_SUM_ROWS, kc), 0) == 0).astype(BF16)

    def update(state, s, si, jc):
        m, acc = state
        hh, _ = streams[si]
        m_new = jnp.maximum(m, jnp.max(s, axis=0, keepdims=True))
        alpha = jnp.exp2(m - m_new)
        p = jnp.exp2(s - m_new)
        vt = vt_ref[0, 0, jc, hh * MLA_V:(hh + 1) * MLA_V, :]
        lhs = jnp.concatenate([vt, ones_rows], axis=0)
        acc = alpha * acc + _dot(lhs, p.astype(BF16))
        return m_new, acc

    def run(units, state):
        state = list(state)
        pending = {}
        for n in range(len(units) + MLA_LOOKAHEAD):
            if n < len(units):
                pending[n] = scores(*units[n])
            if n >= MLA_LOOKAHEAD:
                si, jc, _ = units[n - MLA_LOOKAHEAD]
                state[si] = update(state[si], pending.pop(n - MLA_LOOKAHEAD), si, jc)
        return tuple(state)

    per_iter = MLA_KB // kc

    def full_block(j, carry):
        return run([(si, j * per_iter + i, None) for i in range(per_iter) for si in range(len(streams))], carry)

    init = tuple((jnp.full((1, qs), NEG, F32), jnp.zeros((MLA_V + MLA_SUM_ROWS, qs), F32)) for _ in streams)
    state = lax.fori_loop(0, iq * (t // MLA_KB), full_block, init)
    diag = []
    for i in range(n_chunk):
        for si, (hh, u) in enumerate(streams):
            if i * kc < (u + 1) * qs:
                fully_visible = (i + 1) * kc <= u * qs
                diag.append((si, iq * n_chunk + i, None if fully_visible else i * kc))
    state = run(diag, state)
    for u in range(n_sub):
        per_head = [state[streams.index((hh, u))] for hh in range(2)]
        o_t = jnp.concatenate([acc[:MLA_V] / acc[MLA_V:MLA_V + 1] for _, acc in per_head], axis=0)
        o_ref[0, u * qs:(u + 1) * qs, :] = o_t.T


def _mla(qt, km, vtb, layer):
    b, _, s = qt.shape
    t = MLA_T
    return pl.pallas_call(
        _mla_body,
        grid=(b, MLA_HEADS // 2, s // t),
        in_specs=[
            pl.BlockSpec((1, 2 * LANES, t), lambda bi, hp, iq: (bi, hp, iq)),
            pl.BlockSpec((1, s, 2 * LANES), lambda bi, hp, iq: (bi, 0, hp)),
            pl.BlockSpec((1, 1, s // MLA_KC, 2 * MLA_V, MLA_KC), lambda bi, hp, iq: (bi, hp, 0, 0, 0)),
        ],
        out_specs=pl.BlockSpec((1, t, LANES), lambda bi, hp, iq: (bi, iq, hp)),
        out_shape=jax.ShapeDtypeStruct((b, s, MLA_WIDTH), F32),
        compiler_params=_params("parallel", "parallel", "arbitrary"),
        name=f"mla_l{layer}",
    )(qt, km, vtb)


def _dil_body(q1, q4, q16, k1, k4, k16, k1p, k4p, k16p, v1, v4, v16, v1p, v4p, v16p,
              o_ref, kk1, kk4, kk16, vv1, vv4, vv16, o_s1, o_s4, o_s16, l_s1, l_s4, l_s16):
    c = pl.program_id(2)
    w = W_SUB
    qi = lax.broadcasted_iota(jnp.int32, (w, 2 * w), 0)
    ki = lax.broadcasted_iota(jnp.int32, (w, 2 * w), 1)
    dist = qi + w - ki
    band = (dist >= 0) & (dist <= w)
    lane_q = lax.broadcasted_iota(jnp.int32, (w, LANES), 1)
    head_sel = [lane_q < DIL_HEAD_DIM, lane_q >= DIL_HEAD_DIM]
    qmul = [jnp.where(sel, DIL_HEAD_DIM ** -0.5, 0.0).astype(BF16) for sel in head_sel]
    o_scr = (o_s1, o_s4, o_s16)
    l_scr = (l_s1, l_s4, l_s16)

    branches = ((1, q1, k1, k1p, v1, v1p, kk1, vv1),
                (4, q4, k4, k4p, v4, v4p, kk4, vv4),
                (16, q16, k16, k16p, v16, v16p, kk16, vv16))
    for bi, (d, q_ref, k_ref, kp_ref, v_ref, vp_ref, kk, vv) in enumerate(branches):
        n_blk = DIL_CHUNK // (d * w)
        kk[0:w, :] = kp_ref[0, 0]
        kk[w:, :] = k_ref[0, 0]
        vv[0:w, :] = vp_ref[0, 0]
        vv[w:, :] = v_ref[0, 0]

        def block(n, r, d=d, q_ref=q_ref, kk=kk, vv=vv, bi=bi):
            ls = slice(r * LANES, (r + 1) * LANES)
            r0 = n * w if isinstance(n, int) else pl.multiple_of(n * w, w)
            qb = q_ref[0, 0, pl.ds(r0, w), ls]
            kb = kk[pl.ds(r0, 2 * w), ls]
            vb = vv[pl.ds(r0, 2 * w), ls]
            valid = band & (ki >= jnp.where((c == 0) & (n == 0), w, 0))
            outs, lses = [], []
            for hh in range(2):
                s = jnp.where(valid, _dot_nt(qb * qmul[hh], kb), NEG)
                m = jnp.max(s, axis=-1, keepdims=True)
                e = jnp.exp(s - m)
                den = jnp.sum(e, axis=-1, keepdims=True)
                lses.append(m + jnp.log(den))
                outs.append(_dot((e / den).astype(BF16), vb))
            o = jnp.where(head_sel[0], outs[0], outs[1])
            lse = jnp.where(head_sel[0], lses[0], lses[1])
            if d == 1:
                rows = pl.ds(r0, w)
            else:
                rows = pl.ds(n * (w * d) + r, w, stride=d)
            o_scr[bi][rows, :] = o
            l_scr[bi][rows, :] = lse

        for r in range(d):
            for n in range(n_blk):
                block(n, r)

    l0, l1, l2 = l_s1[...], l_s4[...], l_s16[...]
    m = jnp.maximum(jnp.maximum(l0, l1), l2)
    e0, e1, e2 = jnp.exp(l0 - m), jnp.exp(l1 - m), jnp.exp(l2 - m)
    tot = e0 + e1 + e2
    o_ref[0] = (e0 / tot) * o_s1[...] + (e1 / tot) * o_s4[...] + (e2 / tot) * o_s16[...]


def _dilated(q_in, k_in, v_in, layer):
    b, hp, s, _ = q_in[0].shape
    w = W_SUB
    n_chunk = s // DIL_CHUNK
    cur_specs, prev_specs, scratch_k = [], [], []
    for _, d in DIL_CONFIGS:
        rows, width = DIL_CHUNK // d, d * LANES
        cur_specs.append(pl.BlockSpec((1, 1, rows, width), lambda bi, p, c: (bi, p, c, 0)))
        per = rows // w
        prev_specs.append(pl.BlockSpec(
            (1, 1, w, width), lambda bi, p, c, per=per: (bi, p, jnp.maximum(c * per - 1, 0), 0)))
        scratch_k.append(pltpu.VMEM((rows + w, width), BF16))
    return pl.pallas_call(
        _dil_body,
        grid=(b, hp, n_chunk),
        in_specs=cur_specs + cur_specs + prev_specs + cur_specs + prev_specs,
        out_specs=pl.BlockSpec((1, DIL_CHUNK, LANES), lambda bi, p, c: (bi, c, p)),
        out_shape=jax.ShapeDtypeStruct((b, s, DIL_WIDTH), F32),
        scratch_shapes=scratch_k + scratch_k + [pltpu.VMEM((DIL_CHUNK, LANES), F32)] * 6,
        compiler_params=_params("parallel", "parallel", "arbitrary"),
        name=f"dilated_l{layer}",
    )(*q_in, *k_in, *k_in, *v_in, *v_in)


def _outproj_body(h_ref, om_ref, od_ref, gains_ref, gg_ref, wo_ref, o_ref):
    a = _rms(om_ref[...], gg_ref[:, :MLA_WIDTH])
    b = _rms(od_ref[...], gg_ref[:, MLA_WIDTH:])
    merged = jnp.concatenate([a, b], axis=-1).astype(BF16)
    mix = _dot(merged, wo_ref[...])
    o_ref[...] = h_ref[...] + _rms(mix, gains_ref[3:4, :])


def _outproj(h, om, od, gains, gg, wo, layer):
    n = h.shape[0]
    return pl.pallas_call(
        _outproj_body,
        grid=(n // TM,),
        in_specs=[
            pl.BlockSpec((TM, D_MODEL), lambda t: (t, 0)),
            pl.BlockSpec((TM, MLA_WIDTH), lambda t: (t, 0)),
            pl.BlockSpec((TM, DIL_WIDTH), lambda t: (t, 0)),
            _const_spec((None, 8, D_MODEL), (layer, 0, 0)),
            _const_spec((None, 1, D_MODEL), (layer, 0, 0)),
            _const_spec((None, D_MODEL, D_MODEL), (layer, 0, 0)),
        ],
        out_specs=pl.BlockSpec((TM, D_MODEL), lambda t: (t, 0)),
        out_shape=jax.ShapeDtypeStruct(h.shape, F32),
        compiler_params=_params("parallel"),
        name=f"outproj_l{layer}",
    )(h, om, od, gains, gg, wo)


def _ple_body(h_ref, p_ref, gains_ref, wp_ref, wgate_ref, o_ref):
    h = h_ref[...]
    gate = jax.nn.sigmoid(_dot(_rms(h, gains_ref[6:7, :]).astype(BF16), wgate_ref[...]))
    e = _dot(p_ref[0].astype(BF16), wp_ref[...])
    o_ref[...] = h + _rms(e * gate, gains_ref[7:8, :])


def _ple(h, p3, gains, wp, wgate, layer):
    n = h.shape[0]
    return pl.pallas_call(
        _ple_body,
        grid=(n // TM,),
        in_specs=[
            pl.BlockSpec((TM, D_MODEL), lambda t: (t, 0)),
            pl.BlockSpec((1, TM, PLE_DIM), lambda t: (layer, t, 0)),
            _const_spec((None, 8, D_MODEL), (layer, 0, 0)),
            _const_spec((None, PLE_DIM, D_MODEL), (layer, 0, 0)),
            _const_spec((None, D_MODEL, D_MODEL), (layer, 0, 0)),
        ],
        out_specs=pl.BlockSpec((TM, D_MODEL), lambda t: (t, 0)),
        out_shape=jax.ShapeDtypeStruct(h.shape, F32),
        compiler_params=_params("parallel"),
        name=f"ple_l{layer}",
    )(h, p3, gains, wp, wgate)


def _prep_weights(w_in, w_q_up, w_kv_up):
    depth = w_in.shape[0]
    c1 = Q_LORA + KV_LORA
    c2 = c1 + MLA_ROPE
    zeros = lambda n: jnp.zeros((depth, D_MODEL, n), w_in.dtype)
    win = jnp.concatenate(
        [w_in[..., :c2], zeros(LANES - MLA_ROPE), w_in[..., c2:]], axis=-1).astype(BF16)
    wq = w_q_up.reshape(depth, Q_LORA, MLA_HEADS, MLA_QK)
    wq = jnp.concatenate([wq[..., :MLA_NOPE].reshape(depth, Q_LORA, MLA_HEADS * MLA_NOPE),
                          wq[..., MLA_NOPE:].reshape(depth, Q_LORA, MLA_HEADS * MLA_ROPE)],
                         axis=-1).astype(BF16)
    wkv = w_kv_up.reshape(depth, KV_LORA, MLA_HEADS, MLA_NOPE + MLA_V)
    wk = jnp.pad(wkv[..., :MLA_NOPE], ((0, 0), (0, 0), (0, 0), (0, LANES - MLA_NOPE)))
    wk = wk.reshape(depth, KV_LORA, MLA_HEADS * LANES)
    wv = wkv[..., MLA_NOPE:].reshape(depth, KV_LORA, MLA_WIDTH)
    wkv = jnp.concatenate([wk, wv], axis=-1).astype(BF16)
    return win, wq, wkv


def _rope_tables(positions):
    pos = positions.astype(F32)[..., None]
    shape = positions.shape

    def cos_sin(dim):
        inv = ROPE_THETA ** (-jnp.arange(0, dim, 2, dtype=F32) / dim)
        ang = pos * inv
        return jnp.cos(ang), jnp.sin(ang)

    one = lambda n: jnp.ones(shape + (n,), F32)
    zero = lambda n: jnp.zeros(shape + (n,), F32)
    cat = lambda xs: jnp.concatenate(xs, axis=-1)
    cm, sm = cos_sin(MLA_ROPE)
    cd, sd = cos_sin(DIL_HEAD_DIM)
    cm, sm, cd, sd = lax.optimization_barrier((cm, sm, cd, sd))
    hm, hd = MLA_ROPE // 2, DIL_HEAD_DIM // 2
    rep = LANES // MLA_ROPE
    return jnp.stack([
        cat([cm, cm] * rep),
        cat([-sm, zero(hm)] * rep),
        cat([zero(hm), sm] * rep),
        cat([cd, cd, cd, cd]),
        cat([-sd, zero(hd), -sd, zero(hd)]),
        cat([zero(hd), sd, zero(hd), sd]),
    ], axis=0)


def kernel(x, p, positions, norm_gains, w_in, q_norm, w_q_up, kv_norm, w_kv_up, group_out_norm, w_out,
           ffn_gate, ffn_up, ffn_down, w_ple, w_ple_gate):
    b, s, d = x.shape
    depth = norm_gains.shape[0]
    n = b * s
    assert s % DIL_CHUNK == 0 and s % MLA_T == 0 and s % TM == 0 and d == D_MODEL

    tab = _rope_tables(positions)
    win, wq, wkv = _prep_weights(w_in, w_q_up, w_kv_up)
    wo = w_out.astype(BF16)
    wg, wu, wd = ffn_gate.astype(BF16), ffn_up.astype(BF16), ffn_down.astype(BF16)
    wp, wpg = w_ple.astype(BF16), w_ple_gate.astype(BF16)
    p3 = p.reshape(depth, n, PLE_DIM)
    q_norm = q_norm.reshape(depth, 1, Q_LORA)
    kv_norm = kv_norm.reshape(depth, 1, KV_LORA)
    group_out_norm = group_out_norm.reshape(depth, 1, D_MODEL)

    h = x.reshape(n, d)
    for i in range(depth):
        h = _ffn(h, norm_gains, wg, wu, wd, i, 0)
        proj = _inproj(h.reshape(b, s, d), norm_gains, win, q_norm, wq, kv_norm, wkv, tab, i)
        om = _mla(*proj[:3], i)
        od = _dilated(proj[3:6], proj[6:9], proj[9:12], i)
        h = _outproj(h, om.reshape(n, MLA_WIDTH), od.reshape(n, DIL_WIDTH), norm_gains, group_out_norm, wo, i)
        h = _ffn(h, norm_gains, wg, wu, wd, i, 1)
        h = _ple(h, p3, norm_gains, wp, wpg, i)
    return h.reshape(b, s, d)
```

```python
import jax
import jax.numpy as jnp
from jax import lax
from jax.experimental import pallas as pl
from jax.experimental.pallas import tpu as pltpu

F32 = jnp.float32
BF16 = jnp.bfloat16

D_MODEL = 1024
PLE_DIM = 256
MLA_HEADS = 8
MLA_NOPE = 64
MLA_ROPE = 32
MLA_V = 64
Q_LORA = 256
KV_LORA = 128
DIL_HEADS = 8
DIL_HEAD_DIM = 64
DIL_CONFIGS = ((128, 1), (512, 4), (2048, 16))
D_FF = 2816
MACARON = 0.5
ROPE_THETA = 10000.0
EPS = 1e-6
NEG = -1e30

LANES = 128
MLA_QK = MLA_NOPE + MLA_ROPE
MLA_WIDTH = MLA_HEADS * MLA_V
DIL_WIDTH = DIL_HEADS * DIL_HEAD_DIM
HEAD_PAIRS = DIL_HEADS // 2
N_IN_PAD = 2048
W_SUB = 128
DIL_CHUNK = 2048

TM = 512
MLA_T = 1024
MLA_KB = 1024
MLA_EXP2_SCALE = (MLA_QK ** -0.5) * 1.4426950408889634
MLA_QSUB = 256
MLA_KC = 256
MLA_SUM_ROWS = 16
MLA_LOOKAHEAD = 6
FF_CHUNKS = ((0, 512), (512, 1024), (1024, 1536), (1536, 2048), (2048, 2560), (2560, 2816))
VMEM_LIMIT = 56 * 1024 * 1024


def _rms(x, g):
    ms = jnp.mean(x * x, axis=-1, keepdims=True)
    return x * lax.rsqrt(ms + EPS) * g


def _dot(a, b):
    return jnp.dot(a, b, preferred_element_type=F32)


def _dot_nt(a, b):
    return lax.dot_general(a, b, (((1,), (1,)), ((), ())), preferred_element_type=F32)


def _const_spec(shape, index):
    return pl.BlockSpec(shape, lambda *_: index, pipeline_mode=pl.Buffered(1))


def _params(*sem):
    return pltpu.CompilerParams(dimension_semantics=sem, vmem_limit_bytes=VMEM_LIMIT)


def _ffn_math(h, gains_ref, gi, wg_ref, wu_ref, wd_ref):
    xn = _rms(h, gains_ref[gi:gi + 1, :]).astype(BF16)
    acc = None
    for c0, c1 in FF_CHUNKS:
        g = _dot(xn, wg_ref[:, c0:c1])
        u = _dot(xn, wu_ref[:, c0:c1])
        a = (g * jax.nn.sigmoid(g) * u).astype(BF16)
        part = _dot(a, wd_ref[c0:c1, :])
        acc = part if acc is None else acc + part
    return h + MACARON * _rms(acc, gains_ref[gi + 1:gi + 2, :])


def _ffn_body(h_ref, gains_ref, wg_ref, wu_ref, wd_ref, o_ref):
    o_ref[...] = _ffn_math(h_ref[...], gains_ref, 0, wg_ref, wu_ref, wd_ref)


def _ffn(h, gains, wg, wu, wd, layer):
    n = h.shape[0]
    which = 0
    return pl.pallas_call(
        _ffn_body,
        grid=(n // TM,),
        in_specs=[
            pl.BlockSpec((TM, D_MODEL), lambda t: (t, 0)),
            _const_spec((None, 8, D_MODEL), (layer, 0, 0)),
            _const_spec((None, None, D_MODEL, D_FF), (layer, which, 0, 0)),
            _const_spec((None, None, D_MODEL, D_FF), (layer, which, 0, 0)),
            _const_spec((None, None, D_FF, D_MODEL), (layer, which, 0, 0)),
        ],
        out_specs=pl.BlockSpec((TM, D_MODEL), lambda t: (t, 0)),
        out_shape=jax.ShapeDtypeStruct(h.shape, F32),
        compiler_params=_params("parallel"),
        name=f"ffn_l{layer}",
    )(h, gains, wg, wu, wd)


def _rope(x, c, sa, sb, half):
    return x * c + pltpu.roll(x, LANES - half, 1) * sa + pltpu.roll(x, half, 1) * sb


def _inproj_body(h_ref, gains_ref, win_ref, qn_ref, wq_ref, kvn_ref, wkv_ref, tab_ref,
                 qt_ref, km_ref, vt_ref, *rest):
    n_br = len(DIL_CONFIGS)
    qd_refs, kd_refs, vd_refs = rest[:n_br], rest[n_br:2 * n_br], rest[2 * n_br:3 * n_br]
    stage_ref, stage4_ref = rest[3 * n_br:]
    assert [d for _, d in DIL_CONFIGS] == [1, 4, 16]
    h = h_ref[0]
    hn = _rms(h, gains_ref[2:3, :]).astype(BF16)
    z = _dot(hn, win_ref[...])
    cm, sam, sbm = tab_ref[0, 0], tab_ref[1, 0], tab_ref[2, 0]
    cd, sad, sbd = tab_ref[3, 0], tab_ref[4, 0], tab_ref[5, 0]
    hm = MLA_ROPE // 2
    hd = DIL_HEAD_DIM // 2

    qn = _rms(z[:, :Q_LORA], qn_ref[...]).astype(BF16)
    q = _dot(qn, wq_ref[...])
    n_nope = MLA_HEADS * MLA_NOPE
    nope_t = (q[:, :n_nope] * MLA_EXP2_SCALE).T
    rope_t = jnp.concatenate(
        [(_rope(q[:, n_nope + g * LANES:n_nope + (g + 1) * LANES], cm, sam, sbm, hm) * MLA_EXP2_SCALE).T
         for g in range(MLA_HEADS * MLA_ROPE // LANES)], axis=0)
    for hh in range(MLA_HEADS):
        r0 = hh * LANES
        qt_ref[0, r0:r0 + MLA_NOPE, :] = nope_t[hh * MLA_NOPE:(hh + 1) * MLA_NOPE].astype(BF16)
        qt_ref[0, r0 + MLA_NOPE:r0 + MLA_QK, :] = rope_t[hh * MLA_ROPE:(hh + 1) * MLA_ROPE].astype(BF16)
        qt_ref[0, r0 + MLA_QK:r0 + LANES, :] = jnp.zeros((LANES - MLA_QK, TM), BF16)

    kvn = _rms(z[:, Q_LORA:Q_LORA + KV_LORA], kvn_ref[...]).astype(BF16)
    kv = _dot(kvn, wkv_ref[...])
    kr = pltpu.roll(_rope(z[:, 384:512], cm, sam, sbm, hm), MLA_NOPE, 1)
    for hh in range(MLA_HEADS):
        sl = slice(hh * LANES, (hh + 1) * LANES)
        km_ref[0, :, sl] = (kv[:, sl] + kr).astype(BF16)
    v = kv[:, MLA_HEADS * LANES:]
    for p in range(MLA_HEADS // 2):
        for tc in range(TM // MLA_KC):
            tile = v[tc * MLA_KC:(tc + 1) * MLA_KC, p * LANES:(p + 1) * LANES]
            vt_ref[0, p, tc] = tile.T.astype(BF16)

    for which, out_refs in enumerate((qd_refs, kd_refs, vd_refs)):
        for p in range(HEAD_PAIRS):
            x = z[:, 512 * (which + 1) + p * LANES:512 * (which + 1) + (p + 1) * LANES]
            if which < 2:
                x = _rope(x, cd, sad, sbd, hd)
            stage_ref[which, p] = x
            out_ref1, out_ref4, out_ref16 = out_refs
            out_ref1[0, p] = x.astype(BF16)
            for r in range(4):
                rows = stage_ref[which, p, pl.ds(r, TM // 4, stride=4), :]
                out_ref4[0, p, :, r * LANES:(r + 1) * LANES] = rows.astype(BF16)
                stage4_ref[which, p, r] = rows
            for r in range(4):
                for j in range(4):
                    rows = stage4_ref[which, p, r, pl.ds(j, TM // 16, stride=4), :]
                    out_ref16[0, p, :, (4 * j + r) * LANES:(4 * j + r + 1) * LANES] = rows.astype(BF16)


def _inproj(h3, gains, win, qnorm, wq, kvnorm, wkv, tab, layer):
    b, s, _ = h3.shape
    tok = lambda bi, t: (bi, t, 0)
    pair = lambda bi, t: (bi, 0, t, 0)
    dil_specs = [pl.BlockSpec((1, HEAD_PAIRS, TM // d, d * LANES), pair) for _, d in DIL_CONFIGS]
    dil_shapes = [jax.ShapeDtypeStruct((b, HEAD_PAIRS, s // d, d * LANES), BF16) for _, d in DIL_CONFIGS]
    return pl.pallas_call(
        _inproj_body,
        grid=(b, s // TM),
        in_specs=[
            pl.BlockSpec((1, TM, D_MODEL), tok),
            _const_spec((None, 8, D_MODEL), (layer, 0, 0)),
            _const_spec((None, D_MODEL, N_IN_PAD), (layer, 0, 0)),
            _const_spec((None, 1, Q_LORA), (layer, 0, 0)),
            _const_spec((None, Q_LORA, MLA_HEADS * MLA_QK), (layer, 0, 0)),
            _const_spec((None, 1, KV_LORA), (layer, 0, 0)),
            _const_spec((None, KV_LORA, MLA_HEADS * LANES + MLA_WIDTH), (layer, 0, 0)),
            pl.BlockSpec((6, 1, TM, LANES), lambda bi, t: (0, bi, t, 0)),
        ],
        out_specs=[
            pl.BlockSpec((1, MLA_HEADS * LANES, TM), lambda bi, t: (bi, 0, t)),
            pl.BlockSpec((1, TM, MLA_HEADS * LANES), tok),
            pl.BlockSpec((1, MLA_HEADS // 2, TM // MLA_KC, 2 * MLA_V, MLA_KC), lambda bi, t: (bi, 0, t, 0, 0)),
        ] + dil_specs * 3,
        out_shape=[
            jax.ShapeDtypeStruct((b, MLA_HEADS * LANES, s), BF16),
            jax.ShapeDtypeStruct((b, s, MLA_HEADS * LANES), BF16),
            jax.ShapeDtypeStruct((b, MLA_HEADS // 2, s // MLA_KC, 2 * MLA_V, MLA_KC), BF16),
        ] + dil_shapes * 3,
        scratch_shapes=[pltpu.VMEM((3, HEAD_PAIRS, TM, LANES), F32),
                        pltpu.VMEM((3, HEAD_PAIRS, 4, TM // 4, LANES), F32)],
        compiler_params=_params("parallel", "parallel"),
        name=f"inproj_l{layer}",
    )(h3, gains, win, qnorm, wq, kvnorm, wkv, tab)


def _mla_body(qt_ref, k_ref, vt_ref, o_ref):
    iq = pl.program_id(2)
    t, qs, kc = MLA_T, MLA_QSUB, MLA_KC
    n_sub = t // qs
    n_chunk = t // kc
    key_i = lax.broadcasted_iota(jnp.int32, (kc, qs), 0)
    qry_i = lax.broadcasted_iota(jnp.int32, (kc, qs), 1)
    streams = [(hh, u) for hh in range(2) for u in range(n_sub)]
    qts = {(hh, u): qt_ref[0, hh * LANES:(hh + 1) * LANES, u * qs:(u + 1) * qs] for hh, u in streams}

    def scores(si, jc, key_off):
        hh, u = streams[si]
        k = k_ref[0, pl.ds(pl.multiple_of(jc * kc, kc), kc), hh * LANES:(hh + 1) * LANES]
        s = _dot(k, qts[(hh, u)])
        if key_off is not None:
            s = jnp.where(key_i + (key_off - u * qs) <= qry_i, s, NEG)
        return s

    ones_rows = (lax.broadcasted_iota(jnp.int32, (MLA_SUM_ROWS, kc), 0) == 0).astype(BF16)

    def update(state, s, si, jc):
        m, acc = state
        hh, _ = streams[si]
        m_new = jnp.maximum(m, jnp.max(s, axis=0, keepdims=True))
        alpha = jnp.exp2(m - m_new)
        p = jnp.exp2(s - m_new)
        vt = vt_ref[0, 0, jc, hh * MLA_V:(hh + 1) * MLA_V, :]
        lhs = jnp.concatenate([vt, ones_rows], axis=0)
        acc = alpha * acc + _dot(lhs, p.astype(BF16))
        return m_new, acc

    def run(units, state):
        state = list(state)
        pending = {}
        for n in range(len(units) + MLA_LOOKAHEAD):
            if n < len(units):
                pending[n] = scores(*units[n])
            if n >= MLA_LOOKAHEAD:
                si, jc, _ = units[n - MLA_LOOKAHEAD]
                state[si] = update(state[si], pending.pop(n - MLA_LOOKAHEAD), si, jc)
        return tuple(state)

    per_iter = MLA_KB // kc

    def full_block(j, carry):
        return run([(si, j * per_iter + i, None) for i in range(per_iter) for si in range(len(streams))], carry)

    init = tuple((jnp.full((1, qs), NEG, F32), jnp.zeros((MLA_V + MLA_SUM_ROWS, qs), F32)) for _ in streams)
    state = lax.fori_loop(0, iq * (t // MLA_KB), full_block, init)
    diag = []
    for i in range(n_chunk):
        for si, (hh, u) in enumerate(streams):
            if i * kc < (u + 1) * qs:
                fully_visible = (i + 1) * kc <= u * qs
                diag.append((si, iq * n_chunk + i, None if fully_visible else i * kc))
    state = run(diag, state)
    for u in range(n_sub):
        per_head = [state[streams.index((hh, u))] for hh in range(2)]
        o_t = jnp.concatenate([acc[:MLA_V] / acc[MLA_V:MLA_V + 1] for _, acc in per_head], axis=0)
        o_ref[0, u * qs:(u + 1) * qs, :] = o_t.T


def _mla(qt, km, vtb, layer):
    b, _, s = qt.shape
    t = MLA_T
    return pl.pallas_call(
        _mla_body,
        grid=(b, MLA_HEADS // 2, s // t),
        in_specs=[
            pl.BlockSpec((1, 2 * LANES, t), lambda bi, hp, iq: (bi, hp, iq)),
            pl.BlockSpec((1, s, 2 * LANES), lambda bi, hp, iq: (bi, 0, hp)),
            pl.BlockSpec((1, 1, s // MLA_KC, 2 * MLA_V, MLA_KC), lambda bi, hp, iq: (bi, hp, 0, 0, 0)),
        ],
        out_specs=pl.BlockSpec((1, t, LANES), lambda bi, hp, iq: (bi, iq, hp)),
        out_shape=jax.ShapeDtypeStruct((b, s, MLA_WIDTH), F32),
        compiler_params=_params("parallel", "parallel", "arbitrary"),
        name=f"mla_l{layer}",
    )(qt, km, vtb)


def _dil_body(q1, q4, q16, k1, k4, k16, k1p, k4p, k16p, v1, v4, v16, v1p, v4p, v16p,
              o_ref, kk1, kk4, kk16, vv1, vv4, vv16, o_s1, o_s4, o_s16, l_s1, l_s4, l_s16):
    c = pl.program_id(2)
    w = W_SUB
    qi = lax.broadcasted_iota(jnp.int32, (w, 2 * w), 0)
    ki = lax.broadcasted_iota(jnp.int32, (w, 2 * w), 1)
    dist = qi + w - ki
    band = (dist >= 0) & (dist <= w)
    lane_q = lax.broadcasted_iota(jnp.int32, (w, LANES), 1)
    head_sel = [lane_q < DIL_HEAD_DIM, lane_q >= DIL_HEAD_DIM]
    qmul = [jnp.where(sel, DIL_HEAD_DIM ** -0.5, 0.0).astype(BF16) for sel in head_sel]
    o_scr = (o_s1, o_s4, o_s16)
    l_scr = (l_s1, l_s4, l_s16)

    branches = ((1, q1, k1, k1p, v1, v1p, kk1, vv1),
                (4, q4, k4, k4p, v4, v4p, kk4, vv4),
                (16, q16, k16, k16p, v16, v16p, kk16, vv16))
    for bi, (d, q_ref, k_ref, kp_ref, v_ref, vp_ref, kk, vv) in enumerate(branches):
        n_blk = DIL_CHUNK // (d * w)
        kk[0:w, :] = kp_ref[0, 0]
        kk[w:, :] = k_ref[0, 0]
        vv[0:w, :] = vp_ref[0, 0]
        vv[w:, :] = v_ref[0, 0]

        def block(n, r, d=d, q_ref=q_ref, kk=kk, vv=vv, bi=bi):
            ls = slice(r * LANES, (r + 1) * LANES)
            r0 = n * w if isinstance(n, int) else pl.multiple_of(n * w, w)
            qb = q_ref[0, 0, pl.ds(r0, w), ls]
            kb = kk[pl.ds(r0, 2 * w), ls]
            vb = vv[pl.ds(r0, 2 * w), ls]
            valid = band & (ki >= jnp.where((c == 0) & (n == 0), w, 0))
            outs, lses = [], []
            for hh in range(2):
                s = jnp.where(valid, _dot_nt(qb * qmul[hh], kb), NEG)
                m = jnp.max(s, axis=-1, keepdims=True)
                e = jnp.exp(s - m)
                den = jnp.sum(e, axis=-1, keepdims=True)
                lses.append(m + jnp.log(den))
                outs.append(_dot((e / den).astype(BF16), vb))
            o = jnp.where(head_sel[0], outs[0], outs[1])
            lse = jnp.where(head_sel[0], lses[0], lses[1])
            if d == 1:
                rows = pl.ds(r0, w)
            else:
                rows = pl.ds(n * (w * d) + r, w, stride=d)
            o_scr[bi][rows, :] = o
            l_scr[bi][rows, :] = lse

        for r in range(d):
            for n in range(n_blk):
                block(n, r)

    l0, l1, l2 = l_s1[...], l_s4[...], l_s16[...]
    m = jnp.maximum(jnp.maximum(l0, l1), l2)
    e0, e1, e2 = jnp.exp(l0 - m), jnp.exp(l1 - m), jnp.exp(l2 - m)
    tot = e0 + e1 + e2
    o_ref[0] = (e0 / tot) * o_s1[...] + (e1 / tot) * o_s4[...] + (e2 / tot) * o_s16[...]


def _dilated(q_in, k_in, v_in, layer):
    b, hp, s, _ = q_in[0].shape
    w = W_SUB
    n_chunk = s // DIL_CHUNK
    cur_specs, prev_specs, scratch_k = [], [], []
    for _, d in DIL_CONFIGS:
        rows, width = DIL_CHUNK // d, d * LANES
        cur_specs.append(pl.BlockSpec((1, 1, rows, width), lambda bi, p, c: (bi, p, c, 0)))
        per = rows // w
        prev_specs.append(pl.BlockSpec(
            (1, 1, w, width), lambda bi, p, c, per=per: (bi, p, jnp.maximum(c * per - 1, 0), 0)))
        scratch_k.append(pltpu.VMEM((rows + w, width), BF16))
    return pl.pallas_call(
        _dil_body,
        grid=(b, hp, n_chunk),
        in_specs=cur_specs + cur_specs + prev_specs + cur_specs + prev_specs,
        out_specs=pl.BlockSpec((1, DIL_CHUNK, LANES), lambda bi, p, c: (bi, c, p)),
        out_shape=jax.ShapeDtypeStruct((b, s, DIL_WIDTH), F32),
        scratch_shapes=scratch_k + scratch_k + [pltpu.VMEM((DIL_CHUNK, LANES), F32)] * 6,
        compiler_params=_params("parallel", "parallel", "arbitrary"),
        name=f"dilated_l{layer}",
    )(*q_in, *k_in, *k_in, *v_in, *v_in)


def _post_body(h_ref, om_ref, od_ref, p_ref, gains_ref, gg_ref, wo_ref, wg_ref, wu_ref, wd_ref,
               wp_ref, wgate_ref, o_ref):
    a = _rms(om_ref[...], gg_ref[:, :MLA_WIDTH])
    b = _rms(od_ref[...], gg_ref[:, MLA_WIDTH:])
    merged = jnp.concatenate([a, b], axis=-1).astype(BF16)
    h = h_ref[...] + _rms(_dot(merged, wo_ref[...]), gains_ref[3:4, :])
    h = _ffn_math(h, gains_ref, 4, wg_ref, wu_ref, wd_ref)
    gate = jax.nn.sigmoid(_dot(_rms(h, gains_ref[6:7, :]).astype(BF16), wgate_ref[...]))
    e = _dot(p_ref[0].astype(BF16), wp_ref[...])
    o_ref[...] = h + _rms(e * gate, gains_ref[7:8, :])


def _post(h, om, od, p3, gains, gg, wo, wg, wu, wd, wp, wgate, layer):
    n = h.shape[0]
    tok = lambda t: (t, 0)
    return pl.pallas_call(
        _post_body,
        grid=(n // TM,),
        in_specs=[
            pl.BlockSpec((TM, D_MODEL), tok),
            pl.BlockSpec((TM, MLA_WIDTH), tok),
            pl.BlockSpec((TM, DIL_WIDTH), tok),
            pl.BlockSpec((1, TM, PLE_DIM), lambda t: (layer, t, 0)),
            _const_spec((None, 8, D_MODEL), (layer, 0, 0)),
            _const_spec((None, 1, D_MODEL), (layer, 0, 0)),
            _const_spec((None, D_MODEL, D_MODEL), (layer, 0, 0)),
            _const_spec((None, None, D_MODEL, D_FF), (layer, 1, 0, 0)),
            _const_spec((None, None, D_MODEL, D_FF), (layer, 1, 0, 0)),
            _const_spec((None, None, D_FF, D_MODEL), (layer, 1, 0, 0)),
            _const_spec((None, PLE_DIM, D_MODEL), (layer, 0, 0)),
            _const_spec((None, D_MODEL, D_MODEL), (layer, 0, 0)),
        ],
        out_specs=pl.BlockSpec((TM, D_MODEL), tok),
        out_shape=jax.ShapeDtypeStruct(h.shape, F32),
        compiler_params=_params("parallel"),
        name=f"post_l{layer}",
    )(h, om, od, p3, gains, gg, wo, wg, wu, wd, wp, wgate)


def _prep_weights(w_in, w_q_up, w_kv_up):
    depth = w_in.shape[0]
    c1 = Q_LORA + KV_LORA
    c2 = c1 + MLA_ROPE
    zeros = lambda n: jnp.zeros((depth, D_MODEL, n), w_in.dtype)
    win = jnp.concatenate(
        [w_in[..., :c2], zeros(LANES - MLA_ROPE), w_in[..., c2:]], axis=-1).astype(BF16)
    wq = w_q_up.reshape(depth, Q_LORA, MLA_HEADS, MLA_QK)
    wq = jnp.concatenate([wq[..., :MLA_NOPE].reshape(depth, Q_LORA, MLA_HEADS * MLA_NOPE),
                          wq[..., MLA_NOPE:].reshape(depth, Q_LORA, MLA_HEADS * MLA_ROPE)],
                         axis=-1).astype(BF16)
    wkv = w_kv_up.reshape(depth, KV_LORA, MLA_HEADS, MLA_NOPE + MLA_V)
    wk = jnp.pad(wkv[..., :MLA_NOPE], ((0, 0), (0, 0), (0, 0), (0, LANES - MLA_NOPE)))
    wk = wk.reshape(depth, KV_LORA, MLA_HEADS * LANES)
    wv = wkv[..., MLA_NOPE:].reshape(depth, KV_LORA, MLA_WIDTH)
    wkv = jnp.concatenate([wk, wv], axis=-1).astype(BF16)
    return win, wq, wkv


def _rope_tables(positions):
    pos = positions.astype(F32)[..., None]
    shape = positions.shape

    def cos_sin(dim):
        inv = ROPE_THETA ** (-jnp.arange(0, dim, 2, dtype=F32) / dim)
        ang = pos * inv
        return jnp.cos(ang), jnp.sin(ang)

    zero = lambda n: jnp.zeros(shape + (n,), F32)
    cat = lambda xs: jnp.concatenate(xs, axis=-1)
    cm, sm = cos_sin(MLA_ROPE)
    cd, sd = cos_sin(DIL_HEAD_DIM)
    hm, hd = MLA_ROPE // 2, DIL_HEAD_DIM // 2
    rep = LANES // MLA_ROPE
    return jnp.stack([
        cat([cm, cm] * rep),
        cat([-sm, zero(hm)] * rep),
        cat([zero(hm), sm] * rep),
        cat([cd, cd, cd, cd]),
        cat([-sd, zero(hd), -sd, zero(hd)]),
        cat([zero(hd), sd, zero(hd), sd]),
    ], axis=0)


def kernel(x, p, positions, norm_gains, w_in, q_norm, w_q_up, kv_norm, w_kv_up, group_out_norm, w_out,
           ffn_gate, ffn_up, ffn_down, w_ple, w_ple_gate):
    b, s, d = x.shape
    depth = norm_gains.shape[0]
    n = b * s
    assert s % DIL_CHUNK == 0 and s % MLA_T == 0 and s % TM == 0 and d == D_MODEL

    tab = _rope_tables(positions)
    win, wq, wkv = _prep_weights(w_in, w_q_up, w_kv_up)
    wo = w_out.astype(BF16)
    wg, wu, wd = ffn_gate.astype(BF16), ffn_up.astype(BF16), ffn_down.astype(BF16)
    wp, wpg = w_ple.astype(BF16), w_ple_gate.astype(BF16)
    p3 = p.reshape(depth, n, PLE_DIM)
    q_norm = q_norm.reshape(depth, 1, Q_LORA)
    kv_norm = kv_norm.reshape(depth, 1, KV_LORA)
    group_out_norm = group_out_norm.reshape(depth, 1, D_MODEL)

    h = x.reshape(n, d)
    for i in range(depth):
        h = _ffn(h, norm_gains, wg, wu, wd, i)
        proj = _inproj(h.reshape(b, s, d), norm_gains, win, q_norm, wq, kv_norm, wkv, tab, i)
        om = _mla(*proj[:3], i)
        od = _dilated(proj[3:6], proj[6:9], proj[9:12], i)
        h = _post(h, om.reshape(n, MLA_WIDTH), od.reshape(n, DIL_WIDTH), p3, norm_gains, group_out_norm,
                  wo, wg, wu, wd, wp, wpg, i)
    return h.reshape(b, s, d)
```

```python
import jax
import jax.numpy as jnp
from jax import lax
from jax.experimental import pallas as pl
from jax.experimental.pallas import tpu as pltpu

F32 = jnp.float32
BF16 = jnp.bfloat16

D_MODEL = 1024
PLE_DIM = 256
MLA_HEADS = 8
MLA_NOPE = 64
MLA_ROPE = 32
MLA_V = 64
Q_LORA = 256
KV_LORA = 128
DIL_HEADS = 8
DIL_HEAD_DIM = 64
DIL_CONFIGS = ((128, 1), (512, 4), (2048, 16))
D_FF = 2816
MACARON = 0.5
ROPE_THETA = 10000.0
EPS = 1e-6
NEG = -1e30

LANES = 128
MLA_QK = MLA_NOPE + MLA_ROPE
MLA_WIDTH = MLA_HEADS * MLA_V
DIL_WIDTH = DIL_HEADS * DIL_HEAD_DIM
HEAD_PAIRS = DIL_HEADS // 2
N_IN_PAD = 2048
W_SUB = 128
DIL_CHUNK = 2048

TM = 512
MLA_T = 2048
MLA_KB = 2048
MLA_EXP2_SCALE = (MLA_QK ** -0.5) * 1.4426950408889634
MLA_QSUB = 256
MLA_KC = 256
MLA_SUM_ROWS = 16
MLA_LOOKAHEAD = 6
FF_CHUNKS = ((0, 512), (512, 1024), (1024, 1536), (1536, 2048), (2048, 2560), (2560, 2816))
VMEM_LIMIT = 56 * 1024 * 1024


def _rms(x, g):
    ms = jnp.mean(x * x, axis=-1, keepdims=True)
    return x * lax.rsqrt(ms + EPS) * g


def _dot(a, b):
    return jnp.dot(a, b, preferred_element_type=F32)


def _dot_nt(a, b):
    return lax.dot_general(a, b, (((1,), (1,)), ((), ())), preferred_element_type=F32)


def _const_spec(shape, index):
    return pl.BlockSpec(shape, lambda *_: index, pipeline_mode=pl.Buffered(1))


def _params(*sem):
    return pltpu.CompilerParams(dimension_semantics=sem, vmem_limit_bytes=VMEM_LIMIT)


def _ffn_math(h, gains_ref, gi, wg_ref, wu_ref, wd_ref):
    xn = _rms(h, gains_ref[gi:gi + 1, :]).astype(BF16)
    acc = None
    for c0, c1 in FF_CHUNKS:
        g = _dot(xn, wg_ref[:, c0:c1])
        u = _dot(xn, wu_ref[:, c0:c1])
        a = (g * jax.nn.sigmoid(g) * u).astype(BF16)
        part = _dot(a, wd_ref[c0:c1, :])
        acc = part if acc is None else acc + part
    return h + MACARON * _rms(acc, gains_ref[gi + 1:gi + 2, :])


def _ffn_body(h_ref, gains_ref, wg_ref, wu_ref, wd_ref, o_ref):
    o_ref[...] = _ffn_math(h_ref[...], gains_ref, 0, wg_ref, wu_ref, wd_ref)


def _ffn(h, gains, wg, wu, wd, layer):
    n = h.shape[0]
    which = 0
    return pl.pallas_call(
        _ffn_body,
        grid=(n // TM,),
        in_specs=[
            pl.BlockSpec((TM, D_MODEL), lambda t: (t, 0)),
            _const_spec((None, 8, D_MODEL), (layer, 0, 0)),
            _const_spec((None, None, D_MODEL, D_FF), (layer, which, 0, 0)),
            _const_spec((None, None, D_MODEL, D_FF), (layer, which, 0, 0)),
            _const_spec((None, None, D_FF, D_MODEL), (layer, which, 0, 0)),
        ],
        out_specs=pl.BlockSpec((TM, D_MODEL), lambda t: (t, 0)),
        out_shape=jax.ShapeDtypeStruct(h.shape, F32),
        compiler_params=_params("parallel"),
        name=f"ffn_l{layer}",
    )(h, gains, wg, wu, wd)


def _rope(x, c, sa, sb, half):
    return x * c + pltpu.roll(x, LANES - half, 1) * sa + pltpu.roll(x, half, 1) * sb


def _inproj_body(h_ref, gains_ref, win_ref, qn_ref, wq_ref, kvn_ref, wkv_ref, tab_ref,
                 qt_ref, km_ref, vt_ref, *rest):
    n_br = len(DIL_CONFIGS)
    qd_refs, kd_refs, vd_refs = rest[:n_br], rest[n_br:2 * n_br], rest[2 * n_br:3 * n_br]
    stage_ref, stage4_ref = rest[3 * n_br:]
    assert [d for _, d in DIL_CONFIGS] == [1, 4, 16]
    h = h_ref[0]
    hn = _rms(h, gains_ref[2:3, :]).astype(BF16)
    z = _dot(hn, win_ref[:, :512])
    cm, sam, sbm = tab_ref[0, 0], tab_ref[1, 0], tab_ref[2, 0]
    cd, sad, sbd = tab_ref[3, 0], tab_ref[4, 0], tab_ref[5, 0]
    hm = MLA_ROPE // 2
    hd = DIL_HEAD_DIM // 2

    qn = _rms(z[:, :Q_LORA], qn_ref[...]).astype(BF16)
    q = _dot(qn, wq_ref[...])
    n_nope = MLA_HEADS * MLA_NOPE
    nope_t = (q[:, :n_nope] * MLA_EXP2_SCALE).T
    rope_t = jnp.concatenate(
        [(_rope(q[:, n_nope + g * LANES:n_nope + (g + 1) * LANES], cm, sam, sbm, hm) * MLA_EXP2_SCALE).T
         for g in range(MLA_HEADS * MLA_ROPE // LANES)], axis=0)
    for hh in range(MLA_HEADS):
        r0 = hh * LANES
        qt_ref[0, r0:r0 + MLA_NOPE, :] = nope_t[hh * MLA_NOPE:(hh + 1) * MLA_NOPE].astype(BF16)
        qt_ref[0, r0 + MLA_NOPE:r0 + MLA_QK, :] = rope_t[hh * MLA_ROPE:(hh + 1) * MLA_ROPE].astype(BF16)
        qt_ref[0, r0 + MLA_QK:r0 + LANES, :] = jnp.zeros((LANES - MLA_QK, TM), BF16)

    kvn = _rms(z[:, Q_LORA:Q_LORA + KV_LORA], kvn_ref[...]).astype(BF16)
    kv = _dot(kvn, wkv_ref[...])
    kr = pltpu.roll(_rope(z[:, 384:512], cm, sam, sbm, hm), MLA_NOPE, 1)
    for hh in range(MLA_HEADS):
        sl = slice(hh * LANES, (hh + 1) * LANES)
        km_ref[0, :, sl] = (kv[:, sl] + kr).astype(BF16)
    v = kv[:, MLA_HEADS * LANES:]
    for p in range(MLA_HEADS // 2):
        for tc in range(TM // MLA_KC):
            tile = v[tc * MLA_KC:(tc + 1) * MLA_KC, p * LANES:(p + 1) * LANES]
            vt_ref[0, p, tc] = tile.T.astype(BF16)

    zd = _dot(hn, win_ref[:, 512:])
    for which, out_refs in enumerate((qd_refs, kd_refs, vd_refs)):
        for p in range(HEAD_PAIRS):
            x = zd[:, 512 * which + p * LANES:512 * which + (p + 1) * LANES]
            if which < 2:
                x = _rope(x, cd, sad, sbd, hd)
            stage_ref[which, p] = x
            out_ref1, out_ref4, out_ref16 = out_refs
            out_ref1[0, p] = x.astype(BF16)
            for r in range(4):
                rows = stage_ref[which, p, pl.ds(r, TM // 4, stride=4), :]
                out_ref4[0, p, :, r * LANES:(r + 1) * LANES] = rows.astype(BF16)
                stage4_ref[which, p, r] = rows
            for r in range(4):
                for j in range(4):
                    rows = stage4_ref[which, p, r, pl.ds(j, TM // 16, stride=4), :]
                    out_ref16[0, p, :, (4 * j + r) * LANES:(4 * j + r + 1) * LANES] = rows.astype(BF16)


def _inproj(h3, gains, win, qnorm, wq, kvnorm, wkv, tab, layer):
    b, s, _ = h3.shape
    tok = lambda bi, t: (bi, t, 0)
    pair = lambda bi, t: (bi, 0, t, 0)
    dil_specs = [pl.BlockSpec((1, HEAD_PAIRS, TM // d, d * LANES), pair) for _, d in DIL_CONFIGS]
    dil_shapes = [jax.ShapeDtypeStruct((b, HEAD_PAIRS, s // d, d * LANES), BF16) for _, d in DIL_CONFIGS]
    return pl.pallas_call(
        _inproj_body,
        grid=(b, s // TM),
        in_specs=[
            pl.BlockSpec((1, TM, D_MODEL), tok),
            _const_spec((None, 8, D_MODEL), (layer, 0, 0)),
            _const_spec((None, D_MODEL, N_IN_PAD), (layer, 0, 0)),
            _const_spec((None, 1, Q_LORA), (layer, 0, 0)),
            _const_spec((None, Q_LORA, MLA_HEADS * MLA_QK), (layer, 0, 0)),
            _const_spec((None, 1, KV_LORA), (layer, 0, 0)),
            _const_spec((None, KV_LORA, MLA_HEADS * LANES + MLA_WIDTH), (layer, 0, 0)),
            pl.BlockSpec((6, 1, TM, LANES), lambda bi, t: (0, bi, t, 0)),
        ],
        out_specs=[
            pl.BlockSpec((1, MLA_HEADS * LANES, TM), lambda bi, t: (bi, 0, t)),
            pl.BlockSpec((1, TM, MLA_HEADS * LANES), tok),
            pl.BlockSpec((1, MLA_HEADS // 2, TM // MLA_KC, 2 * MLA_V, MLA_KC), lambda bi, t: (bi, 0, t, 0, 0)),
        ] + dil_specs * 3,
        out_shape=[
            jax.ShapeDtypeStruct((b, MLA_HEADS * LANES, s), BF16),
            jax.ShapeDtypeStruct((b, s, MLA_HEADS * LANES), BF16),
            jax.ShapeDtypeStruct((b, MLA_HEADS // 2, s // MLA_KC, 2 * MLA_V, MLA_KC), BF16),
        ] + dil_shapes * 3,
        scratch_shapes=[pltpu.VMEM((3, HEAD_PAIRS, TM, LANES), F32),
                        pltpu.VMEM((3, HEAD_PAIRS, 4, TM // 4, LANES), F32)],
        compiler_params=_params("parallel", "parallel"),
        name=f"inproj_l{layer}",
    )(h3, gains, win, qnorm, wq, kvnorm, wkv, tab)


def _mla_body(qt_ref, k_ref, vt_ref, o_ref):
    iq = pl.program_id(2)
    t, qs, kc = MLA_T, MLA_QSUB, MLA_KC
    n_sub = t // qs
    n_chunk = t // kc
    key_i = lax.broadcasted_iota(jnp.int32, (kc, qs), 0)
    qry_i = lax.broadcasted_iota(jnp.int32, (kc, qs), 1)
    streams = [(hh, u) for hh in range(2) for u in range(n_sub)]
    qts = {(hh, u): qt_ref[0, hh * LANES:(hh + 1) * LANES, u * qs:(u + 1) * qs] for hh, u in streams}

    def scores(si, jc, key_off):
        hh, u = streams[si]
        k = k_ref[0, pl.ds(pl.multiple_of(jc * kc, kc), kc), hh * LANES:(hh + 1) * LANES]
        s = _dot(k, qts[(hh, u)])
        if key_off is not None:
            s = jnp.where(key_i + (key_off - u * qs) <= qry_i, s, NEG)
        return s

    ones_rows = (lax.broadcasted_iota(jnp.int32, (MLA_SUM_ROWS, kc), 0) == 0).astype(BF16)

    def update(state, s, si, jc):
        m, acc = state
        hh, _ = streams[si]
        m_new = jnp.maximum(m, jnp.max(s, axis=0, keepdims=True))
        alpha = jnp.exp2(m - m_new)
        p = jnp.exp2(s - m_new)
        vt = vt_ref[0, 0, jc, hh * MLA_V:(hh + 1) * MLA_V, :]
        lhs = jnp.concatenate([vt, ones_rows], axis=0)
        acc = alpha * acc + _dot(lhs, p.astype(BF16))
        return m_new, acc

    def run(units, state):
        state = list(state)
        pending = {}
        for n in range(len(units) + MLA_LOOKAHEAD):
            if n < len(units):
                pending[n] = scores(*units[n])
            if n >= MLA_LOOKAHEAD:
                si, jc, _ = units[n - MLA_LOOKAHEAD]
                state[si] = update(state[si], pending.pop(n - MLA_LOOKAHEAD), si, jc)
        return tuple(state)

    per_iter = MLA_KB // kc

    def full_block(j, carry):
        return run([(si, j * per_iter + i, None) for i in range(per_iter) for si in range(len(streams))], carry)

    init = tuple((jnp.full((1, qs), NEG, F32), jnp.zeros((MLA_V + MLA_SUM_ROWS, qs), F32)) for _ in streams)
    state = lax.fori_loop(0, iq * (t // MLA_KB), full_block, init)
    diag = []
    for i in range(n_chunk):
        for si, (hh, u) in enumerate(streams):
            if i * kc < (u + 1) * qs:
                fully_visible = (i + 1) * kc <= u * qs
                diag.append((si, iq * n_chunk + i, None if fully_visible else i * kc))
    state = run(diag, state)
    for u in range(n_sub):
        per_head = [state[streams.index((hh, u))] for hh in range(2)]
        o_t = jnp.concatenate([acc[:MLA_V] / acc[MLA_V:MLA_V + 1] for _, acc in per_head], axis=0)
        o_ref[0, u * qs:(u + 1) * qs, :] = o_t.T


def _mla(qt, km, vtb, layer):
    b, _, s = qt.shape
    t = MLA_T
    return pl.pallas_call(
        _mla_body,
        grid=(b, MLA_HEADS // 2, s // t),
        in_specs=[
            pl.BlockSpec((1, 2 * LANES, t), lambda bi, hp, iq: (bi, hp, iq)),
            pl.BlockSpec((1, s, 2 * LANES), lambda bi, hp, iq: (bi, 0, hp)),
            pl.BlockSpec((1, 1, s // MLA_KC, 2 * MLA_V, MLA_KC), lambda bi, hp, iq: (bi, hp, 0, 0, 0)),
        ],
        out_specs=pl.BlockSpec((1, t, LANES), lambda bi, hp, iq: (bi, iq, hp)),
        out_shape=jax.ShapeDtypeStruct((b, s, MLA_WIDTH), F32),
        compiler_params=_params("parallel", "parallel", "arbitrary"),
        name=f"mla_l{layer}",
    )(qt, km, vtb)


def _dil_body(q1, q4, q16, k1, k4, k16, k1p, k4p, k16p, v1, v4, v16, v1p, v4p, v16p,
              o_ref, kk1, kk4, kk16, vv1, vv4, vv16, o_s1, o_s4, o_s16, l_s1, l_s4, l_s16):
    c = pl.program_id(2)
    w = W_SUB
    qi = lax.broadcasted_iota(jnp.int32, (w, 2 * w), 0)
    ki = lax.broadcasted_iota(jnp.int32, (w, 2 * w), 1)
    dist = qi + w - ki
    band = (dist >= 0) & (dist <= w)
    lane_q = lax.broadcasted_iota(jnp.int32, (w, LANES), 1)
    head_sel = [lane_q < DIL_HEAD_DIM, lane_q >= DIL_HEAD_DIM]
    qmul = [jnp.where(sel, DIL_HEAD_DIM ** -0.5, 0.0).astype(BF16) for sel in head_sel]
    o_scr = (o_s1, o_s4, o_s16)
    l_scr = (l_s1, l_s4, l_s16)

    branches = ((1, q1, k1, k1p, v1, v1p, kk1, vv1),
                (4, q4, k4, k4p, v4, v4p, kk4, vv4),
                (16, q16, k16, k16p, v16, v16p, kk16, vv16))
    for bi, (d, q_ref, k_ref, kp_ref, v_ref, vp_ref, kk, vv) in enumerate(branches):
        n_blk = DIL_CHUNK // (d * w)
        kk[0:w, :] = kp_ref[0, 0]
        kk[w:, :] = k_ref[0, 0]
        vv[0:w, :] = vp_ref[0, 0]
        vv[w:, :] = v_ref[0, 0]

        def block(n, r, d=d, q_ref=q_ref, kk=kk, vv=vv, bi=bi):
            ls = slice(r * LANES, (r + 1) * LANES)
            r0 = n * w if isinstance(n, int) else pl.multiple_of(n * w, w)
            qb = q_ref[0, 0, pl.ds(r0, w), ls]
            kb = kk[pl.ds(r0, 2 * w), ls]
            vb = vv[pl.ds(r0, 2 * w), ls]
            valid = band & (ki >= jnp.where((c == 0) & (n == 0), w, 0))
            outs, lses = [], []
            for hh in range(2):
                s = jnp.where(valid, _dot_nt(qb * qmul[hh], kb), NEG)
                m = jnp.max(s, axis=-1, keepdims=True)
                e = jnp.exp(s - m)
                den = jnp.sum(e, axis=-1, keepdims=True)
                lses.append(m + jnp.log(den))
                outs.append(_dot((e / den).astype(BF16), vb))
            o = jnp.where(head_sel[0], outs[0], outs[1])
            lse = jnp.where(head_sel[0], lses[0], lses[1])
            if d == 1:
                rows = pl.ds(r0, w)
            else:
                rows = pl.ds(n * (w * d) + r, w, stride=d)
            o_scr[bi][rows, :] = o
            l_scr[bi][rows, :] = lse

        for r in range(d):
            for n in range(n_blk):
                block(n, r)

    l0, l1, l2 = l_s1[...], l_s4[...], l_s16[...]
    m = jnp.maximum(jnp.maximum(l0, l1), l2)
    e0, e1, e2 = jnp.exp(l0 - m), jnp.exp(l1 - m), jnp.exp(l2 - m)
    tot = e0 + e1 + e2
    o_ref[0] = (e0 / tot) * o_s1[...] + (e1 / tot) * o_s4[...] + (e2 / tot) * o_s16[...]


def _dilated(q_in, k_in, v_in, layer):
    b, hp, s, _ = q_in[0].shape
    w = W_SUB
    n_chunk = s // DIL_CHUNK
    cur_specs, prev_specs, scratch_k = [], [], []
    for _, d in DIL_CONFIGS:
        rows, width = DIL_CHUNK // d, d * LANES
        cur_specs.append(pl.BlockSpec((1, 1, rows, width), lambda bi, p, c: (bi, p, c, 0)))
        per = rows // w
        prev_specs.append(pl.BlockSpec(
            (1, 1, w, width), lambda bi, p, c, per=per: (bi, p, jnp.maximum(c * per - 1, 0), 0)))
        scratch_k.append(pltpu.VMEM((rows + w, width), BF16))
    return pl.pallas_call(
        _dil_body,
        grid=(b, hp, n_chunk),
        in_specs=cur_specs + cur_specs + prev_specs + cur_specs + prev_specs,
        out_specs=pl.BlockSpec((1, DIL_CHUNK, LANES), lambda bi, p, c: (bi, c, p)),
        out_shape=jax.ShapeDtypeStruct((b, s, DIL_WIDTH), F32),
        scratch_shapes=scratch_k + scratch_k + [pltpu.VMEM((DIL_CHUNK, LANES), F32)] * 6,
        compiler_params=_params("parallel", "parallel", "arbitrary"),
        name=f"dilated_l{layer}",
    )(*q_in, *k_in, *k_in, *v_in, *v_in)


def _post_body(h_ref, om_ref, od_ref, p_ref, gains_ref, gg_ref, wo_ref, wg_ref, wu_ref, wd_ref,
               wp_ref, wgate_ref, o_ref):
    a = _rms(om_ref[...], gg_ref[:, :MLA_WIDTH])
    b = _rms(od_ref[...], gg_ref[:, MLA_WIDTH:])
    merged = jnp.concatenate([a, b], axis=-1).astype(BF16)
    h = h_ref[...] + _rms(_dot(merged, wo_ref[...]), gains_ref[3:4, :])
    h = _ffn_math(h, gains_ref, 4, wg_ref, wu_ref, wd_ref)
    gate = jax.nn.sigmoid(_dot(_rms(h, gains_ref[6:7, :]).astype(BF16), wgate_ref[...]))
    e = _dot(p_ref[0].astype(BF16), wp_ref[...])
    o_ref[...] = h + _rms(e * gate, gains_ref[7:8, :])


def _post(h, om, od, p3, gains, gg, wo, wg, wu, wd, wp, wgate, layer):
    n = h.shape[0]
    tok = lambda t: (t, 0)
    return pl.pallas_call(
        _post_body,
        grid=(n // TM,),
        in_specs=[
            pl.BlockSpec((TM, D_MODEL), tok),
            pl.BlockSpec((TM, MLA_WIDTH), tok),
            pl.BlockSpec((TM, DIL_WIDTH), tok),
            pl.BlockSpec((1, TM, PLE_DIM), lambda t: (layer, t, 0)),
            _const_spec((None, 8, D_MODEL), (layer, 0, 0)),
            _const_spec((None, 1, D_MODEL), (layer, 0, 0)),
            _const_spec((None, D_MODEL, D_MODEL), (layer, 0, 0)),
            _const_spec((None, None, D_MODEL, D_FF), (layer, 1, 0, 0)),
            _const_spec((None, None, D_MODEL, D_FF), (layer, 1, 0, 0)),
            _const_spec((None, None, D_FF, D_MODEL), (layer, 1, 0, 0)),
            _const_spec((None, PLE_DIM, D_MODEL), (layer, 0, 0)),
            _const_spec((None, D_MODEL, D_MODEL), (layer, 0, 0)),
        ],
        out_specs=pl.BlockSpec((TM, D_MODEL), tok),
        out_shape=jax.ShapeDtypeStruct(h.shape, F32),
        compiler_params=_params("parallel"),
        name=f"post_l{layer}",
    )(h, om, od, p3, gains, gg, wo, wg, wu, wd, wp, wgate)


def _prep_weights(w_in, w_q_up, w_kv_up):
    depth = w_in.shape[0]
    c1 = Q_LORA + KV_LORA
    c2 = c1 + MLA_ROPE
    zeros = lambda n: jnp.zeros((depth, D_MODEL, n), w_in.dtype)
    win = jnp.concatenate(
        [w_in[..., :c2], zeros(LANES - MLA_ROPE), w_in[..., c2:]], axis=-1).astype(BF16)
    wq = w_q_up.reshape(depth, Q_LORA, MLA_HEADS, MLA_QK)
    wq = jnp.concatenate([wq[..., :MLA_NOPE].reshape(depth, Q_LORA, MLA_HEADS * MLA_NOPE),
                          wq[..., MLA_NOPE:].reshape(depth, Q_LORA, MLA_HEADS * MLA_ROPE)],
                         axis=-1).astype(BF16)
    wkv = w_kv_up.reshape(depth, KV_LORA, MLA_HEADS, MLA_NOPE + MLA_V)
    wk = jnp.pad(wkv[..., :MLA_NOPE], ((0, 0), (0, 0), (0, 0), (0, LANES - MLA_NOPE)))
    wk = wk.reshape(depth, KV_LORA, MLA_HEADS * LANES)
    wv = wkv[..., MLA_NOPE:].reshape(depth, KV_LORA, MLA_WIDTH)
    wkv = jnp.concatenate([wk, wv], axis=-1).astype(BF16)
    return win, wq, wkv


def _rope_tables(positions):
    pos = positions.astype(F32)[..., None]
    lane = jnp.arange(LANES)

    def patterns(dim):
        half = dim // 2
        inv = ROPE_THETA ** (-jnp.arange(0, dim, 2, dtype=F32) / dim)
        ang = pos * inv[lane % half]
        first = (lane % dim) < half
        sin = jnp.sin(ang)
        return jnp.cos(ang), sin * jnp.where(first, -1.0, 0.0), sin * jnp.where(first, 0.0, 1.0)

    return jnp.stack(patterns(MLA_ROPE) + patterns(DIL_HEAD_DIM), axis=0)


def kernel(x, p, positions, norm_gains, w_in, q_norm, w_q_up, kv_norm, w_kv_up, group_out_norm, w_out,
           ffn_gate, ffn_up, ffn_down, w_ple, w_ple_gate):
    b, s, d = x.shape
    depth = norm_gains.shape[0]
    n = b * s
    assert s % DIL_CHUNK == 0 and s % MLA_T == 0 and s % TM == 0 and d == D_MODEL

    tab = _rope_tables(positions)
    win, wq, wkv = _prep_weights(w_in, w_q_up, w_kv_up)
    wo = w_out.astype(BF16)
    wg, wu, wd = ffn_gate.astype(BF16), ffn_up.astype(BF16), ffn_down.astype(BF16)
    wp, wpg = w_ple.astype(BF16), w_ple_gate.astype(BF16)
    p3 = p.reshape(depth, n, PLE_DIM)
    q_norm = q_norm.reshape(depth, 1, Q_LORA)
    kv_norm = kv_norm.reshape(depth, 1, KV_LORA)
    group_out_norm = group_out_norm.reshape(depth, 1, D_MODEL)

    h = x.reshape(n, d)
    for i in range(depth):
        h = _ffn(h, norm_gains, wg, wu, wd, i)
        proj = _inproj(h.reshape(b, s, d), norm_gains, win, q_norm, wq, kv_norm, wkv, tab, i)
        om = _mla(*proj[:3], i)
        od = _dilated(proj[3:6], proj[6:9], proj[9:12], i)
        h = _post(h, om.reshape(n, MLA_WIDTH), od.reshape(n, DIL_WIDTH), p3, norm_gains, group_out_norm,
                  wo, wg, wu, wd, wp, wpg, i)
    return h.reshape(b, s, d)
```

```python
import jax
import jax.numpy as jnp
from jax import lax
from jax.experimental import pallas as pl
from jax.experimental.pallas import tpu as pltpu

F32 = jnp.float32
BF16 = jnp.bfloat16

D_MODEL = 1024
PLE_DIM = 256
MLA_HEADS = 8
MLA_NOPE = 64
MLA_ROPE = 32
MLA_V = 64
Q_LORA = 256
KV_LORA = 128
DIL_HEADS = 8
DIL_HEAD_DIM = 64
DIL_CONFIGS = ((128, 1), (512, 4), (2048, 16))
D_FF = 2816
MACARON = 0.5
ROPE_THETA = 10000.0
EPS = 1e-6
NEG = -1e30

LANES = 128
MLA_QK = MLA_NOPE + MLA_ROPE
MLA_WIDTH = MLA_HEADS * MLA_V
DIL_WIDTH = DIL_HEADS * DIL_HEAD_DIM
HEAD_PAIRS = DIL_HEADS // 2
N_IN_PAD = 2048
W_SUB = 128
DIL_CHUNK = 2048

TM = 512
MLA_T = 2048
MLA_KB = 2048
MLA_EXP2_SCALE = (MLA_QK ** -0.5) * 1.4426950408889634
MLA_QSUB = 256
MLA_KC = 256
MLA_SUM_ROWS = 16
MLA_LOOKAHEAD = 6
FF_CHUNKS = ((0, 512), (512, 1024), (1024, 1536), (1536, 2048), (2048, 2560), (2560, 2816))
VMEM_LIMIT = 56 * 1024 * 1024


def _rms(x, g):
    ms = jnp.mean(x * x, axis=-1, keepdims=True)
    return x * lax.rsqrt(ms + EPS) * g


def _dot(a, b):
    return jnp.dot(a, b, preferred_element_type=F32)


def _dot_nt(a, b):
    return lax.dot_general(a, b, (((1,), (1,)), ((), ())), preferred_element_type=F32)


def _const_spec(shape, index):
    return pl.BlockSpec(shape, lambda *_: index, pipeline_mode=pl.Buffered(1))


def _params(*sem):
    return pltpu.CompilerParams(dimension_semantics=sem, vmem_limit_bytes=VMEM_LIMIT)


def _ffn_math(h, gains_ref, gi, wg_ref, wu_ref, wd_ref):
    xn = _rms(h, gains_ref[gi:gi + 1, :]).astype(BF16)
    acc = None
    for c0, c1 in FF_CHUNKS:
        g = _dot(xn, wg_ref[:, c0:c1])
        u = _dot(xn, wu_ref[:, c0:c1])
        a = (g * jax.nn.sigmoid(g) * u).astype(BF16)
        part = _dot(a, wd_ref[c0:c1, :])
        acc = part if acc is None else acc + part
    return h + MACARON * _rms(acc, gains_ref[gi + 1:gi + 2, :])


def _ffn_body(h_ref, gains_ref, wg_ref, wu_ref, wd_ref, o_ref):
    o_ref[...] = _ffn_math(h_ref[...], gains_ref, 0, wg_ref, wu_ref, wd_ref)


def _ffn(h, gains, wg, wu, wd, layer):
    n = h.shape[0]
    which = 0
    return pl.pallas_call(
        _ffn_body,
        grid=(n // TM,),
        in_specs=[
            pl.BlockSpec((TM, D_MODEL), lambda t: (t, 0)),
            _const_spec((None, 8, D_MODEL), (layer, 0, 0)),
            _const_spec((None, None, D_MODEL, D_FF), (layer, which, 0, 0)),
            _const_spec((None, None, D_MODEL, D_FF), (layer, which, 0, 0)),
            _const_spec((None, None, D_FF, D_MODEL), (layer, which, 0, 0)),
        ],
        out_specs=pl.BlockSpec((TM, D_MODEL), lambda t: (t, 0)),
        out_shape=jax.ShapeDtypeStruct(h.shape, F32),
        compiler_params=_params("parallel"),
        name=f"ffn_l{layer}",
    )(h, gains, wg, wu, wd)


def _rope(x, cos, sin, half):
    lane = lax.broadcasted_iota(jnp.int32, x.shape, 1)
    first = (lane % (2 * half)) < half
    partner = jnp.where(first, -pltpu.roll(x, LANES - half, 1), pltpu.roll(x, half, 1))
    return x * cos + partner * sin


def _inproj_body(h_ref, gains_ref, win_ref, qn_ref, wq_ref, kvn_ref, wkv_ref, tab_ref,
                 qt_ref, km_ref, vt_ref, *rest):
    n_br = len(DIL_CONFIGS)
    qd_refs, kd_refs, vd_refs = rest[:n_br], rest[n_br:2 * n_br], rest[2 * n_br:3 * n_br]
    stage_ref, stage4_ref = rest[3 * n_br:]
    assert [d for _, d in DIL_CONFIGS] == [1, 4, 16]
    h = h_ref[0]
    hn = _rms(h, gains_ref[2:3, :]).astype(BF16)
    z = _dot(hn, win_ref[:, :512])
    cm, sm, cd, sd = tab_ref[0, 0], tab_ref[1, 0], tab_ref[2, 0], tab_ref[3, 0]
    hm = MLA_ROPE // 2
    hd = DIL_HEAD_DIM // 2

    qn = _rms(z[:, :Q_LORA], qn_ref[...]).astype(BF16)
    q = _dot(qn, wq_ref[...])
    n_nope = MLA_HEADS * MLA_NOPE
    nope_t = (q[:, :n_nope] * MLA_EXP2_SCALE).T
    rope_t = jnp.concatenate(
        [(_rope(q[:, n_nope + g * LANES:n_nope + (g + 1) * LANES], cm, sm, hm) * MLA_EXP2_SCALE).T
         for g in range(MLA_HEADS * MLA_ROPE // LANES)], axis=0)
    for hh in range(MLA_HEADS):
        r0 = hh * LANES
        qt_ref[0, r0:r0 + MLA_NOPE, :] = nope_t[hh * MLA_NOPE:(hh + 1) * MLA_NOPE].astype(BF16)
        qt_ref[0, r0 + MLA_NOPE:r0 + MLA_QK, :] = rope_t[hh * MLA_ROPE:(hh + 1) * MLA_ROPE].astype(BF16)
        qt_ref[0, r0 + MLA_QK:r0 + LANES, :] = jnp.zeros((LANES - MLA_QK, TM), BF16)

    kvn = _rms(z[:, Q_LORA:Q_LORA + KV_LORA], kvn_ref[...]).astype(BF16)
    kv = _dot(kvn, wkv_ref[...])
    kr = pltpu.roll(_rope(z[:, 384:512], cm, sm, hm), MLA_NOPE, 1)
    for hh in range(MLA_HEADS):
        sl = slice(hh * LANES, (hh + 1) * LANES)
        km_ref[0, :, sl] = (kv[:, sl] + kr).astype(BF16)
    v = kv[:, MLA_HEADS * LANES:]
    for p in range(MLA_HEADS // 2):
        for tc in range(TM // MLA_KC):
            tile = v[tc * MLA_KC:(tc + 1) * MLA_KC, p * LANES:(p + 1) * LANES]
            vt_ref[0, p, tc] = tile.T.astype(BF16)

    zd = _dot(hn, win_ref[:, 512:])
    for which, out_refs in enumerate((qd_refs, kd_refs, vd_refs)):
        for p in range(HEAD_PAIRS):
            x = zd[:, 512 * which + p * LANES:512 * which + (p + 1) * LANES]
            if which < 2:
                x = _rope(x, cd, sd, hd)
            stage_ref[which, p] = x
            out_ref1, out_ref4, out_ref16 = out_refs
            out_ref1[0, p] = x.astype(BF16)
            for r in range(4):
                rows = stage_ref[which, p, pl.ds(r, TM // 4, stride=4), :]
                out_ref4[0, p, :, r * LANES:(r + 1) * LANES] = rows.astype(BF16)
                stage4_ref[which, p, r] = rows
            for r in range(4):
                for j in range(4):
                    rows = stage4_ref[which, p, r, pl.ds(j, TM // 16, stride=4), :]
                    out_ref16[0, p, :, (4 * j + r) * LANES:(4 * j + r + 1) * LANES] = rows.astype(BF16)


def _inproj(h3, gains, win, qnorm, wq, kvnorm, wkv, tab, layer):
    b, s, _ = h3.shape
    tok = lambda bi, t: (bi, t, 0)
    pair = lambda bi, t: (bi, 0, t, 0)
    dil_specs = [pl.BlockSpec((1, HEAD_PAIRS, TM // d, d * LANES), pair) for _, d in DIL_CONFIGS]
    dil_shapes = [jax.ShapeDtypeStruct((b, HEAD_PAIRS, s // d, d * LANES), BF16) for _, d in DIL_CONFIGS]
    return pl.pallas_call(
        _inproj_body,
        grid=(b, s // TM),
        in_specs=[
            pl.BlockSpec((1, TM, D_MODEL), tok),
            _const_spec((None, 8, D_MODEL), (layer, 0, 0)),
            _const_spec((None, D_MODEL, N_IN_PAD), (layer, 0, 0)),
            _const_spec((None, 1, Q_LORA), (layer, 0, 0)),
            _const_spec((None, Q_LORA, MLA_HEADS * MLA_QK), (layer, 0, 0)),
            _const_spec((None, 1, KV_LORA), (layer, 0, 0)),
            _const_spec((None, KV_LORA, MLA_HEADS * LANES + MLA_WIDTH), (layer, 0, 0)),
            pl.BlockSpec((4, 1, TM, LANES), lambda bi, t: (0, bi, t, 0)),
        ],
        out_specs=[
            pl.BlockSpec((1, MLA_HEADS * LANES, TM), lambda bi, t: (bi, 0, t)),
            pl.BlockSpec((1, TM, MLA_HEADS * LANES), tok),
            pl.BlockSpec((1, MLA_HEADS // 2, TM // MLA_KC, 2 * MLA_V, MLA_KC), lambda bi, t: (bi, 0, t, 0, 0)),
        ] + dil_specs * 3,
        out_shape=[
            jax.ShapeDtypeStruct((b, MLA_HEADS * LANES, s), BF16),
            jax.ShapeDtypeStruct((b, s, MLA_HEADS * LANES), BF16),
            jax.ShapeDtypeStruct((b, MLA_HEADS // 2, s // MLA_KC, 2 * MLA_V, MLA_KC), BF16),
        ] + dil_shapes * 3,
        scratch_shapes=[pltpu.VMEM((3, HEAD_PAIRS, TM, LANES), F32),
                        pltpu.VMEM((3, HEAD_PAIRS, 4, TM // 4, LANES), F32)],
        compiler_params=_params("parallel", "parallel"),
        name=f"inproj_l{layer}",
    )(h3, gains, win, qnorm, wq, kvnorm, wkv, tab)


def _mla_body(qt_ref, k_ref, vt_ref, o_ref):
    iq = pl.program_id(2)
    t, qs, kc = MLA_T, MLA_QSUB, MLA_KC
    n_sub = t // qs
    n_chunk = t // kc
    key_i = lax.broadcasted_iota(jnp.int32, (kc, qs), 0)
    qry_i = lax.broadcasted_iota(jnp.int32, (kc, qs), 1)
    streams = [(hh, u) for hh in range(2) for u in range(n_sub)]
    qts = {(hh, u): qt_ref[0, hh * LANES:(hh + 1) * LANES, u * qs:(u + 1) * qs] for hh, u in streams}

    def scores(si, jc, key_off):
        hh, u = streams[si]
        k = k_ref[0, pl.ds(pl.multiple_of(jc * kc, kc), kc), hh * LANES:(hh + 1) * LANES]
        s = _dot(k, qts[(hh, u)])
        if key_off is not None:
            s = jnp.where(key_i + (key_off - u * qs) <= qry_i, s, NEG)
        return s

    ones_rows = (lax.broadcasted_iota(jnp.int32, (MLA_SUM_ROWS, kc), 0) == 0).astype(BF16)

    def update(state, s, si, jc):
        m, acc = state
        hh, _ = streams[si]
        m_new = jnp.maximum(m, jnp.max(s, axis=0, keepdims=True))
        alpha = jnp.exp2(m - m_new)
        p = jnp.exp2(s - m_new)
        vt = vt_ref[0, 0, jc, hh * MLA_V:(hh + 1) * MLA_V, :]
        lhs = jnp.concatenate([vt, ones_rows], axis=0)
        acc = alpha * acc + _dot(lhs, p.astype(BF16))
        return m_new, acc

    def run(units, state):
        state = list(state)
        pending = {}
        for n in range(len(units) + MLA_LOOKAHEAD):
            if n < len(units):
                pending[n] = scores(*units[n])
            if n >= MLA_LOOKAHEAD:
                si, jc, _ = units[n - MLA_LOOKAHEAD]
                state[si] = update(state[si], pending.pop(n - MLA_LOOKAHEAD), si, jc)
        return tuple(state)

    per_iter = MLA_KB // kc

    def full_block(j, carry):
        return run([(si, j * per_iter + i, None) for i in range(per_iter) for si in range(len(streams))], carry)

    init = tuple((jnp.full((1, qs), NEG, F32), jnp.zeros((MLA_V + MLA_SUM_ROWS, qs), F32)) for _ in streams)
    state = lax.fori_loop(0, iq * (t // MLA_KB), full_block, init)
    diag = []
    for i in range(n_chunk):
        for si, (hh, u) in enumerate(streams):
            if i * kc < (u + 1) * qs:
                fully_visible = (i + 1) * kc <= u * qs
                diag.append((si, iq * n_chunk + i, None if fully_visible else i * kc))
    state = run(diag, state)
    for u in range(n_sub):
        per_head = [state[streams.index((hh, u))] for hh in range(2)]
        o_t = jnp.concatenate([acc[:MLA_V] / acc[MLA_V:MLA_V + 1] for _, acc in per_head], axis=0)
        o_ref[0, u * qs:(u + 1) * qs, :] = o_t.T


def _mla(qt, km, vtb, layer):
    b, _, s = qt.shape
    t = MLA_T
    return pl.pallas_call(
        _mla_body,
        grid=(b, MLA_HEADS // 2, s // t),
        in_specs=[
            pl.BlockSpec((1, 2 * LANES, t), lambda bi, hp, iq: (bi, hp, iq)),
            pl.BlockSpec((1, s, 2 * LANES), lambda bi, hp, iq: (bi, 0, hp)),
            pl.BlockSpec((1, 1, s // MLA_KC, 2 * MLA_V, MLA_KC), lambda bi, hp, iq: (bi, hp, 0, 0, 0)),
        ],
        out_specs=pl.BlockSpec((1, t, LANES), lambda bi, hp, iq: (bi, iq, hp)),
        out_shape=jax.ShapeDtypeStruct((b, s, MLA_WIDTH), F32),
        compiler_params=_params("parallel", "parallel", "arbitrary"),
        name=f"mla_l{layer}",
    )(qt, km, vtb)


def _dil_body(q1, q4, q16, k1, k4, k16, k1p, k4p, k16p, v1, v4, v16, v1p, v4p, v16p,
              o_ref, kk1, kk4, kk16, vv1, vv4, vv16, o_s1, o_s4, o_s16, l_s1, l_s4, l_s16):
    c = pl.program_id(2)
    w = W_SUB
    qi = lax.broadcasted_iota(jnp.int32, (w, 2 * w), 0)
    ki = lax.broadcasted_iota(jnp.int32, (w, 2 * w), 1)
    dist = qi + w - ki
    band = (dist >= 0) & (dist <= w)
    lane_q = lax.broadcasted_iota(jnp.int32, (w, LANES), 1)
    head_sel = [lane_q < DIL_HEAD_DIM, lane_q >= DIL_HEAD_DIM]
    qmul = [jnp.where(sel, DIL_HEAD_DIM ** -0.5, 0.0).astype(BF16) for sel in head_sel]
    o_scr = (o_s1, o_s4, o_s16)
    l_scr = (l_s1, l_s4, l_s16)

    branches = ((1, q1, k1, k1p, v1, v1p, kk1, vv1),
                (4, q4, k4, k4p, v4, v4p, kk4, vv4),
                (16, q16, k16, k16p, v16, v16p, kk16, vv16))
    for bi, (d, q_ref, k_ref, kp_ref, v_ref, vp_ref, kk, vv) in enumerate(branches):
        n_blk = DIL_CHUNK // (d * w)
        kk[0:w, :] = kp_ref[0, 0]
        kk[w:, :] = k_ref[0, 0]
        vv[0:w, :] = vp_ref[0, 0]
        vv[w:, :] = v_ref[0, 0]

        def block(n, r, d=d, q_ref=q_ref, kk=kk, vv=vv, bi=bi):
            ls = slice(r * LANES, (r + 1) * LANES)
            r0 = n * w if isinstance(n, int) else pl.multiple_of(n * w, w)
            qb = q_ref[0, 0, pl.ds(r0, w), ls]
            kb = kk[pl.ds(r0, 2 * w), ls]
            vb = vv[pl.ds(r0, 2 * w), ls]
            valid = band & (ki >= jnp.where((c == 0) & (n == 0), w, 0))
            outs, lses = [], []
            for hh in range(2):
                s = jnp.where(valid, _dot_nt(qb * qmul[hh], kb), NEG)
                m = jnp.max(s, axis=-1, keepdims=True)
                e = jnp.exp(s - m)
                den = jnp.sum(e, axis=-1, keepdims=True)
                lses.append(m + jnp.log(den))
                outs.append(_dot((e / den).astype(BF16), vb))
            o = jnp.where(head_sel[0], outs[0], outs[1])
            lse = jnp.where(head_sel[0], lses[0], lses[1])
            if d == 1:
                rows = pl.ds(r0, w)
            else:
                rows = pl.ds(n * (w * d) + r, w, stride=d)
            o_scr[bi][rows, :] = o
            l_scr[bi][rows, :] = lse

        for r in range(d):
            for n in range(n_blk):
                block(n, r)

    l0, l1, l2 = l_s1[...], l_s4[...], l_s16[...]
    m = jnp.maximum(jnp.maximum(l0, l1), l2)
    e0, e1, e2 = jnp.exp(l0 - m), jnp.exp(l1 - m), jnp.exp(l2 - m)
    tot = e0 + e1 + e2
    o_ref[0] = (e0 / tot) * o_s1[...] + (e1 / tot) * o_s4[...] + (e2 / tot) * o_s16[...]


def _dilated(q_in, k_in, v_in, layer):
    b, hp, s, _ = q_in[0].shape
    w = W_SUB
    n_chunk = s // DIL_CHUNK
    cur_specs, prev_specs, scratch_k = [], [], []
    for _, d in DIL_CONFIGS:
        rows, width = DIL_CHUNK // d, d * LANES
        cur_specs.append(pl.BlockSpec((1, 1, rows, width), lambda bi, p, c: (bi, p, c, 0)))
        per = rows // w
        prev_specs.append(pl.BlockSpec(
            (1, 1, w, width), lambda bi, p, c, per=per: (bi, p, jnp.maximum(c * per - 1, 0), 0)))
        scratch_k.append(pltpu.VMEM((rows + w, width), BF16))
    return pl.pallas_call(
        _dil_body,
        grid=(b, hp, n_chunk),
        in_specs=cur_specs + cur_specs + prev_specs + cur_specs + prev_specs,
        out_specs=pl.BlockSpec((1, DIL_CHUNK, LANES), lambda bi, p, c: (bi, c, p)),
        out_shape=jax.ShapeDtypeStruct((b, s, DIL_WIDTH), F32),
        scratch_shapes=scratch_k + scratch_k + [pltpu.VMEM((DIL_CHUNK, LANES), F32)] * 6,
        compiler_params=_params("parallel", "parallel", "arbitrary"),
        name=f"dilated_l{layer}",
    )(*q_in, *k_in, *k_in, *v_in, *v_in)


def _post_body(h_ref, om_ref, od_ref, p_ref, gains_ref, gg_ref, wo_ref, wg_ref, wu_ref, wd_ref,
               wp_ref, wgate_ref, o_ref):
    a = _rms(om_ref[...], gg_ref[:, :MLA_WIDTH])
    b = _rms(od_ref[...], gg_ref[:, MLA_WIDTH:])
    merged = jnp.concatenate([a, b], axis=-1).astype(BF16)
    h = h_ref[...] + _rms(_dot(merged, wo_ref[...]), gains_ref[3:4, :])
    h = _ffn_math(h, gains_ref, 4, wg_ref, wu_ref, wd_ref)
    gate = jax.nn.sigmoid(_dot(_rms(h, gains_ref[6:7, :]).astype(BF16), wgate_ref[...]))
    e = _dot(p_ref[0].astype(BF16), wp_ref[...])
    o_ref[...] = h + _rms(e * gate, gains_ref[7:8, :])


def _post(h, om, od, p3, gains, gg, wo, wg, wu, wd, wp, wgate, layer):
    n = h.shape[0]
    tok = lambda t: (t, 0)
    return pl.pallas_call(
        _post_body,
        grid=(n // TM,),
        in_specs=[
            pl.BlockSpec((TM, D_MODEL), tok),
            pl.BlockSpec((TM, MLA_WIDTH), tok),
            pl.BlockSpec((TM, DIL_WIDTH), tok),
            pl.BlockSpec((1, TM, PLE_DIM), lambda t: (layer, t, 0)),
            _const_spec((None, 8, D_MODEL), (layer, 0, 0)),
            _const_spec((None, 1, D_MODEL), (layer, 0, 0)),
            _const_spec((None, D_MODEL, D_MODEL), (layer, 0, 0)),
            _const_spec((None, None, D_MODEL, D_FF), (layer, 1, 0, 0)),
            _const_spec((None, None, D_MODEL, D_FF), (layer, 1, 0, 0)),
            _const_spec((None, None, D_FF, D_MODEL), (layer, 1, 0, 0)),
            _const_spec((None, PLE_DIM, D_MODEL), (layer, 0, 0)),
            _const_spec((None, D_MODEL, D_MODEL), (layer, 0, 0)),
        ],
        out_specs=pl.BlockSpec((TM, D_MODEL), tok),
        out_shape=jax.ShapeDtypeStruct(h.shape, F32),
        compiler_params=_params("parallel"),
        name=f"post_l{layer}",
    )(h, om, od, p3, gains, gg, wo, wg, wu, wd, wp, wgate)


def _prep_weights(w_in, w_q_up, w_kv_up):
    depth = w_in.shape[0]
    c1 = Q_LORA + KV_LORA
    c2 = c1 + MLA_ROPE
    zeros = lambda n: jnp.zeros((depth, D_MODEL, n), w_in.dtype)
    win = jnp.concatenate(
        [w_in[..., :c2], zeros(LANES - MLA_ROPE), w_in[..., c2:]], axis=-1).astype(BF16)
    wq = w_q_up.reshape(depth, Q_LORA, MLA_HEADS, MLA_QK)
    wq = jnp.concatenate([wq[..., :MLA_NOPE].reshape(depth, Q_LORA, MLA_HEADS * MLA_NOPE),
                          wq[..., MLA_NOPE:].reshape(depth, Q_LORA, MLA_HEADS * MLA_ROPE)],
                         axis=-1).astype(BF16)
    wkv = w_kv_up.reshape(depth, KV_LORA, MLA_HEADS, MLA_NOPE + MLA_V)
    wk = jnp.pad(wkv[..., :MLA_NOPE], ((0, 0), (0, 0), (0, 0), (0, LANES - MLA_NOPE)))
    wk = wk.reshape(depth, KV_LORA, MLA_HEADS * LANES)
    wv = wkv[..., MLA_NOPE:].reshape(depth, KV_LORA, MLA_WIDTH)
    wkv = jnp.concatenate([wk, wv], axis=-1).astype(BF16)
    return win, wq, wkv


def _rope_tables(positions):
    pos = positions.astype(F32)[..., None]
    lane = jnp.arange(LANES)

    def patterns(dim):
        half = dim // 2
        inv = ROPE_THETA ** (-jnp.arange(0, dim, 2, dtype=F32) / dim)
        ang = pos * inv[lane % half]
        return jnp.cos(ang), jnp.sin(ang)

    return jnp.stack(patterns(MLA_ROPE) + patterns(DIL_HEAD_DIM), axis=0)


def kernel(x, p, positions, norm_gains, w_in, q_norm, w_q_up, kv_norm, w_kv_up, group_out_norm, w_out,
           ffn_gate, ffn_up, ffn_down, w_ple, w_ple_gate):
    b, s, d = x.shape
    depth = norm_gains.shape[0]
    n = b * s
    assert s % DIL_CHUNK == 0 and s % MLA_T == 0 and s % TM == 0 and d == D_MODEL

    tab = _rope_tables(positions)
    win, wq, wkv = _prep_weights(w_in, w_q_up, w_kv_up)
    wo = w_out.astype(BF16)
    wg, wu, wd = ffn_gate.astype(BF16), ffn_up.astype(BF16), ffn_down.astype(BF16)
    wp, wpg = w_ple.astype(BF16), w_ple_gate.astype(BF16)
    p3 = p.reshape(depth, n, PLE_DIM)
    q_norm = q_norm.reshape(depth, 1, Q_LORA)
    kv_norm = kv_norm.reshape(depth, 1, KV_LORA)
    group_out_norm = group_out_norm.reshape(depth, 1, D_MODEL)

    h = x.reshape(n, d)
    for i in range(depth):
        h = _ffn(h, norm_gains, wg, wu, wd, i)
        proj = _inproj(h.reshape(b, s, d), norm_gains, win, q_norm, wq, kv_norm, wkv, tab, i)
        om = _mla(*proj[:3], i)
        od = _dilated(proj[3:6], proj[6:9], proj[9:12], i)
        h = _post(h, om.reshape(n, MLA_WIDTH), od.reshape(n, DIL_WIDTH), p3, norm_gains, group_out_norm,
                  wo, wg, wu, wd, wp, wpg, i)
    return h.reshape(b, s, d)
```

```python
import jax
import jax.numpy as jnp
from jax import lax
from jax.experimental import pallas as pl
from jax.experimental.pallas import tpu as pltpu

F32 = jnp.float32
BF16 = jnp.bfloat16

D_MODEL = 1024
PLE_DIM = 256
MLA_HEADS = 8
MLA_NOPE = 64
MLA_ROPE = 32
MLA_V = 64
Q_LORA = 256
KV_LORA = 128
DIL_HEADS = 8
DIL_HEAD_DIM = 64
DIL_CONFIGS = ((128, 1), (512, 4), (2048, 16))
D_FF = 2816
MACARON = 0.5
ROPE_THETA = 10000.0
EPS = 1e-6
NEG = -1e30

LANES = 128
MLA_QK = MLA_NOPE + MLA_ROPE
MLA_WIDTH = MLA_HEADS * MLA_V
DIL_WIDTH = DIL_HEADS * DIL_HEAD_DIM
HEAD_PAIRS = DIL_HEADS // 2
N_IN_PAD = 2048
W_SUB = 128
DIL_CHUNK = 2048

TM = 512
MLA_T = 2048
MLA_KB = 2048
MLA_EXP2_SCALE = (MLA_QK ** -0.5) * 1.4426950408889634
MLA_QSUB = 256
MLA_KC = 256
MLA_SUM_ROWS = 16
MLA_LOOKAHEAD = 6
FF_CHUNKS = ((0, 512), (512, 1024), (1024, 1536), (1536, 2048), (2048, 2560), (2560, 2816))
VMEM_LIMIT = 56 * 1024 * 1024


def _rms(x, g):
    ms = jnp.mean(x * x, axis=-1, keepdims=True)
    return x * lax.rsqrt(ms + EPS) * g


def _dot(a, b):
    return jnp.dot(a, b, preferred_element_type=F32)


def _dot_nt(a, b):
    return lax.dot_general(a, b, (((1,), (1,)), ((), ())), preferred_element_type=F32)


def _const_spec(shape, index):
    return pl.BlockSpec(shape, lambda *_: index, pipeline_mode=pl.Buffered(1))


def _params(*sem):
    return pltpu.CompilerParams(dimension_semantics=sem, vmem_limit_bytes=VMEM_LIMIT)


def _ffn_math(h, gains_ref, gi, wg_ref, wu_ref, wd_ref):
    xn = _rms(h, gains_ref[gi:gi + 1, :]).astype(BF16)
    acc = None
    for c0, c1 in FF_CHUNKS:
        g = _dot(xn, wg_ref[:, c0:c1])
        u = _dot(xn, wu_ref[:, c0:c1])
        a = (g * jax.nn.sigmoid(g) * u).astype(BF16)
        part = _dot(a, wd_ref[c0:c1, :])
        acc = part if acc is None else acc + part
    return h + MACARON * _rms(acc, gains_ref[gi + 1:gi + 2, :])


def _ffn_body(h_ref, gains_ref, wg_ref, wu_ref, wd_ref, o_ref):
    o_ref[...] = _ffn_math(h_ref[...], gains_ref, 0, wg_ref, wu_ref, wd_ref)


def _ffn(h, gains, wg, wu, wd, layer):
    n = h.shape[0]
    which = 0
    return pl.pallas_call(
        _ffn_body,
        grid=(n // TM,),
        in_specs=[
            pl.BlockSpec((TM, D_MODEL), lambda t: (t, 0)),
            _const_spec((None, 8, D_MODEL), (layer, 0, 0)),
            _const_spec((None, None, D_MODEL, D_FF), (layer, which, 0, 0)),
            _const_spec((None, None, D_MODEL, D_FF), (layer, which, 0, 0)),
            _const_spec((None, None, D_FF, D_MODEL), (layer, which, 0, 0)),
        ],
        out_specs=pl.BlockSpec((TM, D_MODEL), lambda t: (t, 0)),
        out_shape=jax.ShapeDtypeStruct(h.shape, F32),
        compiler_params=_params("parallel"),
        name=f"ffn_l{layer}",
    )(h, gains, wg, wu, wd)


def _rope(x, cos, sin, half):
    lane = lax.broadcasted_iota(jnp.int32, x.shape, 1)
    first = (lane % (2 * half)) < half
    partner = jnp.where(first, -pltpu.roll(x, LANES - half, 1), pltpu.roll(x, half, 1))
    return x * cos + partner * sin


def _inproj_body(h_ref, gains_ref, win_ref, qn_ref, wq_ref, kvn_ref, wkv_ref, tab_ref,
                 qt_ref, km_ref, vt_ref, *rest):
    n_br = len(DIL_CONFIGS)
    qd_refs, kd_refs, vd_refs = rest[:n_br], rest[n_br:2 * n_br], rest[2 * n_br:3 * n_br]
    stage_ref, stage4_ref = rest[3 * n_br:]
    assert [d for _, d in DIL_CONFIGS] == [1, 4, 16]
    h = h_ref[0]
    hn = _rms(h, gains_ref[2:3, :]).astype(BF16)
    z = _dot(hn, win_ref[:, :512])
    cm, sm, cd, sd = tab_ref[0, 0], tab_ref[1, 0], tab_ref[2, 0], tab_ref[3, 0]
    hm = MLA_ROPE // 2
    hd = DIL_HEAD_DIM // 2

    qn = _rms(z[:, :Q_LORA], qn_ref[...]).astype(BF16)
    q = _dot(qn, wq_ref[...])
    n_nope = MLA_HEADS * MLA_NOPE
    nope_t = (q[:, :n_nope] * MLA_EXP2_SCALE).T
    rope_t = jnp.concatenate(
        [(_rope(q[:, n_nope + g * LANES:n_nope + (g + 1) * LANES], cm, sm, hm) * MLA_EXP2_SCALE).T
         for g in range(MLA_HEADS * MLA_ROPE // LANES)], axis=0)
    for hh in range(MLA_HEADS):
        r0 = hh * LANES
        qt_ref[0, r0:r0 + MLA_NOPE, :] = nope_t[hh * MLA_NOPE:(hh + 1) * MLA_NOPE].astype(BF16)
        qt_ref[0, r0 + MLA_NOPE:r0 + MLA_QK, :] = rope_t[hh * MLA_ROPE:(hh + 1) * MLA_ROPE].astype(BF16)
        qt_ref[0, r0 + MLA_QK:r0 + LANES, :] = jnp.zeros((LANES - MLA_QK, TM), BF16)

    kvn = _rms(z[:, Q_LORA:Q_LORA + KV_LORA], kvn_ref[...]).astype(BF16)
    kv = _dot(kvn, wkv_ref[...])
    kr = pltpu.roll(_rope(z[:, 384:512], cm, sm, hm), MLA_NOPE, 1)
    for hh in range(MLA_HEADS):
        sl = slice(hh * LANES, (hh + 1) * LANES)
        km_ref[0, :, sl] = (kv[:, sl] + kr).astype(BF16)
    v = kv[:, MLA_HEADS * LANES:]
    for p in range(MLA_HEADS // 2):
        for tc in range(TM // MLA_KC):
            tile = v[tc * MLA_KC:(tc + 1) * MLA_KC, p * LANES:(p + 1) * LANES]
            vt_ref[0, p, tc] = tile.T.astype(BF16)

    zd = _dot(hn, win_ref[:, 512:])
    for which, out_refs in enumerate((qd_refs, kd_refs, vd_refs)):
        for p in range(HEAD_PAIRS):
            x = zd[:, 512 * which + p * LANES:512 * which + (p + 1) * LANES]
            if which < 2:
                x = _rope(x, cd, sd, hd)
            stage_ref[which, p] = x
            out_ref1, out_ref4, out_ref16 = out_refs
            out_ref1[0, p] = x.astype(BF16)
            for r in range(4):
                rows = stage_ref[which, p, pl.ds(r, TM // 4, stride=4), :]
                out_ref4[0, p, :, r * LANES:(r + 1) * LANES] = rows.astype(BF16)
                stage4_ref[which, p, r] = rows
            for r in range(4):
                for j in range(4):
                    rows = stage4_ref[which, p, r, pl.ds(j, TM // 16, stride=4), :]
                    out_ref16[0, p, :, (4 * j + r) * LANES:(4 * j + r + 1) * LANES] = rows.astype(BF16)


def _inproj(h3, gains, win, qnorm, wq, kvnorm, wkv, tab, layer):
    b, s, _ = h3.shape
    tok = lambda bi, t: (bi, t, 0)
    pair = lambda bi, t: (bi, 0, t, 0)
    dil_specs = [pl.BlockSpec((1, HEAD_PAIRS, TM // d, d * LANES), pair) for _, d in DIL_CONFIGS]
    dil_shapes = [jax.ShapeDtypeStruct((b, HEAD_PAIRS, s // d, d * LANES), BF16) for _, d in DIL_CONFIGS]
    return pl.pallas_call(
        _inproj_body,
        grid=(b, s // TM),
        in_specs=[
            pl.BlockSpec((1, TM, D_MODEL), tok),
            _const_spec((None, 8, D_MODEL), (layer, 0, 0)),
            _const_spec((None, D_MODEL, N_IN_PAD), (layer, 0, 0)),
            _const_spec((None, 1, Q_LORA), (layer, 0, 0)),
            _const_spec((None, Q_LORA, MLA_HEADS * MLA_QK), (layer, 0, 0)),
            _const_spec((None, 1, KV_LORA), (layer, 0, 0)),
            _const_spec((None, KV_LORA, MLA_HEADS * LANES + MLA_WIDTH), (layer, 0, 0)),
            pl.BlockSpec((4, 1, TM, LANES), lambda bi, t: (0, bi, t, 0)),
        ],
        out_specs=[
            pl.BlockSpec((1, MLA_HEADS * LANES, TM), lambda bi, t: (bi, 0, t)),
            pl.BlockSpec((1, TM, MLA_HEADS * LANES), tok),
            pl.BlockSpec((1, MLA_HEADS // 2, TM // MLA_KC, 2 * MLA_V, MLA_KC), lambda bi, t: (bi, 0, t, 0, 0)),
        ] + dil_specs * 3,
        out_shape=[
            jax.ShapeDtypeStruct((b, MLA_HEADS * LANES, s), BF16),
            jax.ShapeDtypeStruct((b, s, MLA_HEADS * LANES), BF16),
            jax.ShapeDtypeStruct((b, MLA_HEADS // 2, s // MLA_KC, 2 * MLA_V, MLA_KC), BF16),
        ] + dil_shapes * 3,
        scratch_shapes=[pltpu.VMEM((3, HEAD_PAIRS, TM, LANES), F32),
                        pltpu.VMEM((3, HEAD_PAIRS, 4, TM // 4, LANES), F32)],
        compiler_params=_params("parallel", "parallel"),
        name=f"inproj_l{layer}",
    )(h3, gains, win, qnorm, wq, kvnorm, wkv, tab)


def _mla_body(qt_ref, k_ref, vt_ref, o_ref):
    iq = pl.program_id(2)
    t, qs, kc = MLA_T, MLA_QSUB, MLA_KC
    n_sub = t // qs
    n_chunk = t // kc
    key_i = lax.broadcasted_iota(jnp.int32, (kc, qs), 0)
    qry_i = lax.broadcasted_iota(jnp.int32, (kc, qs), 1)
    streams = [(hh, u) for hh in range(2) for u in range(n_sub)]
    qts = {(hh, u): qt_ref[0, hh * LANES:(hh + 1) * LANES, u * qs:(u + 1) * qs] for hh, u in streams}

    def scores(si, jc, key_off):
        hh, u = streams[si]
        k = k_ref[0, pl.ds(pl.multiple_of(jc * kc, kc), kc), hh * LANES:(hh + 1) * LANES]
        s = _dot(k, qts[(hh, u)])
        if key_off is not None:
            s = jnp.where(key_i + (key_off - u * qs) <= qry_i, s, NEG)
        return s

    ones_rows = (lax.broadcasted_iota(jnp.int32, (MLA_SUM_ROWS, kc), 0) == 0).astype(BF16)

    def update(state, s, si, jc):
        m, acc = state
        hh, _ = streams[si]
        m_new = jnp.maximum(m, jnp.max(s, axis=0, keepdims=True))
        alpha = jnp.exp2(m - m_new)
        p = jnp.exp2(s - m_new)
        vt = vt_ref[0, 0, jc, hh * MLA_V:(hh + 1) * MLA_V, :]
        lhs = jnp.concatenate([vt, ones_rows], axis=0)
        acc = alpha * acc + _dot(lhs, p.astype(BF16))
        return m_new, acc

    def run(units, state):
        state = list(state)
        pending = {}
        for n in range(len(units) + MLA_LOOKAHEAD):
            if n < len(units):
                pending[n] = scores(*units[n])
            if n >= MLA_LOOKAHEAD:
                si, jc, _ = units[n - MLA_LOOKAHEAD]
                state[si] = update(state[si], pending.pop(n - MLA_LOOKAHEAD), si, jc)
        return tuple(state)

    per_iter = MLA_KB // kc

    def full_block(j, carry):
        return run([(si, j * per_iter + i, None) for i in range(per_iter) for si in range(len(streams))], carry)

    init = tuple((jnp.full((1, qs), NEG, F32), jnp.zeros((MLA_V + MLA_SUM_ROWS, qs), F32)) for _ in streams)
    state = lax.fori_loop(0, iq * (t // MLA_KB), full_block, init)
    diag = []
    for i in range(n_chunk):
        for si, (hh, u) in enumerate(streams):
            if i * kc < (u + 1) * qs:
                fully_visible = (i + 1) * kc <= u * qs
                diag.append((si, iq * n_chunk + i, None if fully_visible else i * kc))
    state = run(diag, state)
    for u in range(n_sub):
        per_head = [state[streams.index((hh, u))] for hh in range(2)]
        o_t = jnp.concatenate([acc[:MLA_V] / acc[MLA_V:MLA_V + 1] for _, acc in per_head], axis=0)
        o_ref[0, u * qs:(u + 1) * qs, :] = o_t.T


def _mla(qt, km, vtb, layer):
    b, _, s = qt.shape
    t = MLA_T
    return pl.pallas_call(
        _mla_body,
        grid=(b, MLA_HEADS // 2, s // t),
        in_specs=[
            pl.BlockSpec((1, 2 * LANES, t), lambda bi, hp, iq: (bi, hp, iq)),
            pl.BlockSpec((1, s, 2 * LANES), lambda bi, hp, iq: (bi, 0, hp)),
            pl.BlockSpec((1, 1, s // MLA_KC, 2 * MLA_V, MLA_KC), lambda bi, hp, iq: (bi, hp, 0, 0, 0)),
        ],
        out_specs=pl.BlockSpec((1, t, LANES), lambda bi, hp, iq: (bi, iq, hp)),
        out_shape=jax.ShapeDtypeStruct((b, s, MLA_WIDTH), F32),
        compiler_params=_params("parallel", "parallel", "arbitrary"),
        name=f"mla_l{layer}",
    )(qt, km, vtb)


def _dil_body(q1, q4, q16, k1, k4, k16, k1p, k4p, k16p, v1, v4, v16, v1p, v4p, v16p,
              o_ref, kk1, kk4, kk16, vv1, vv4, vv16, o_s1, o_s4, o_s16, l_s1, l_s4, l_s16):
    c = pl.program_id(2)
    w = W_SUB
    qi = lax.broadcasted_iota(jnp.int32, (w, 2 * w), 0)
    ki = lax.broadcasted_iota(jnp.int32, (w, 2 * w), 1)
    dist = qi + w - ki
    band = (dist >= 0) & (dist <= w)
    lane_q = lax.broadcasted_iota(jnp.int32, (w, LANES), 1)
    head_sel = [lane_q < DIL_HEAD_DIM, lane_q >= DIL_HEAD_DIM]
    qmul = [jnp.where(sel, DIL_HEAD_DIM ** -0.5, 0.0).astype(BF16) for sel in head_sel]
    o_scr = (o_s1, o_s4, o_s16)
    l_scr = (l_s1, l_s4, l_s16)

    branches = ((1, q1, k1, k1p, v1, v1p, kk1, vv1),
                (4, q4, k4, k4p, v4, v4p, kk4, vv4),
                (16, q16, k16, k16p, v16, v16p, kk16, vv16))
    for bi, (d, q_ref, k_ref, kp_ref, v_ref, vp_ref, kk, vv) in enumerate(branches):
        n_blk = DIL_CHUNK // (d * w)
        kk[0:w, :] = kp_ref[0, 0]
        kk[w:, :] = k_ref[0, 0]
        vv[0:w, :] = vp_ref[0, 0]
        vv[w:, :] = v_ref[0, 0]

        def block(n, r, d=d, q_ref=q_ref, kk=kk, vv=vv, bi=bi):
            ls = slice(r * LANES, (r + 1) * LANES)
            r0 = n * w if isinstance(n, int) else pl.multiple_of(n * w, w)
            qb = q_ref[0, 0, pl.ds(r0, w), ls]
            kb = kk[pl.ds(r0, 2 * w), ls]
            vb = vv[pl.ds(r0, 2 * w), ls]
            valid = band & (ki >= jnp.where(c == 0, w, 0)) if n == 0 else band
            outs, lses = [], []
            for hh in range(2):
                s = jnp.where(valid, _dot_nt(qb * qmul[hh], kb), NEG)
                m = jnp.max(s, axis=-1, keepdims=True)
                e = jnp.exp(s - m)
                den = jnp.sum(e, axis=-1, keepdims=True)
                lses.append(m + jnp.log(den))
                outs.append(_dot((e / den).astype(BF16), vb))
            o = jnp.where(head_sel[0], outs[0], outs[1])
            lse = jnp.where(head_sel[0], lses[0], lses[1])
            if d == 1:
                rows = pl.ds(r0, w)
            else:
                rows = pl.ds(n * (w * d) + r, w, stride=d)
            o_scr[bi][rows, :] = o
            l_scr[bi][rows, :] = lse

        for r in range(d):
            for n in range(n_blk):
                block(n, r)

    l0, l1, l2 = l_s1[...], l_s4[...], l_s16[...]
    m = jnp.maximum(jnp.maximum(l0, l1), l2)
    e0, e1, e2 = jnp.exp(l0 - m), jnp.exp(l1 - m), jnp.exp(l2 - m)
    tot = e0 + e1 + e2
    o_ref[0] = (e0 / tot) * o_s1[...] + (e1 / tot) * o_s4[...] + (e2 / tot) * o_s16[...]


def _dilated(q_in, k_in, v_in, layer):
    b, hp, s, _ = q_in[0].shape
    w = W_SUB
    n_chunk = s // DIL_CHUNK
    cur_specs, prev_specs, scratch_k = [], [], []
    for _, d in DIL_CONFIGS:
        rows, width = DIL_CHUNK // d, d * LANES
        cur_specs.append(pl.BlockSpec((1, 1, rows, width), lambda bi, p, c: (bi, p, c, 0)))
        per = rows // w
        prev_specs.append(pl.BlockSpec(
            (1, 1, w, width), lambda bi, p, c, per=per: (bi, p, jnp.maximum(c * per - 1, 0), 0)))
        scratch_k.append(pltpu.VMEM((rows + w, width), BF16))
    return pl.pallas_call(
        _dil_body,
        grid=(b, hp, n_chunk),
        in_specs=cur_specs + cur_specs + prev_specs + cur_specs + prev_specs,
        out_specs=pl.BlockSpec((1, DIL_CHUNK, LANES), lambda bi, p, c: (bi, c, p)),
        out_shape=jax.ShapeDtypeStruct((b, s, DIL_WIDTH), F32),
        scratch_shapes=scratch_k + scratch_k + [pltpu.VMEM((DIL_CHUNK, LANES), F32)] * 6,
        compiler_params=_params("parallel", "parallel", "arbitrary"),
        name=f"dilated_l{layer}",
    )(*q_in, *k_in, *k_in, *v_in, *v_in)


def _post_body(h_ref, om_ref, od_ref, p_ref, gains_ref, gg_ref, wo_ref, wg_ref, wu_ref, wd_ref,
               wp_ref, wgate_ref, o_ref):
    e = _dot(p_ref[0].astype(BF16), wp_ref[...])
    a = _rms(om_ref[...], gg_ref[:, :MLA_WIDTH])
    b = _rms(od_ref[...], gg_ref[:, MLA_WIDTH:])
    merged = jnp.concatenate([a, b], axis=-1).astype(BF16)
    h = h_ref[...] + _rms(_dot(merged, wo_ref[...]), gains_ref[3:4, :])
    h = _ffn_math(h, gains_ref, 4, wg_ref, wu_ref, wd_ref)
    gate = jax.nn.sigmoid(_dot(_rms(h, gains_ref[6:7, :]).astype(BF16), wgate_ref[...]))
    o_ref[...] = h + _rms(e * gate, gains_ref[7:8, :])


def _post(h, om, od, p3, gains, gg, wo, wg, wu, wd, wp, wgate, layer):
    n = h.shape[0]
    tok = lambda t: (t, 0)
    return pl.pallas_call(
        _post_body,
        grid=(n // TM,),
        in_specs=[
            pl.BlockSpec((TM, D_MODEL), tok),
            pl.BlockSpec((TM, MLA_WIDTH), tok),
            pl.BlockSpec((TM, DIL_WIDTH), tok),
            pl.BlockSpec((1, TM, PLE_DIM), lambda t: (layer, t, 0)),
            _const_spec((None, 8, D_MODEL), (layer, 0, 0)),
            _const_spec((None, 1, D_MODEL), (layer, 0, 0)),
            _const_spec((None, D_MODEL, D_MODEL), (layer, 0, 0)),
            _const_spec((None, None, D_MODEL, D_FF), (layer, 1, 0, 0)),
            _const_spec((None, None, D_MODEL, D_FF), (layer, 1, 0, 0)),
            _const_spec((None, None, D_FF, D_MODEL), (layer, 1, 0, 0)),
            _const_spec((None, PLE_DIM, D_MODEL), (layer, 0, 0)),
            _const_spec((None, D_MODEL, D_MODEL), (layer, 0, 0)),
        ],
        out_specs=pl.BlockSpec((TM, D_MODEL), tok),
        out_shape=jax.ShapeDtypeStruct(h.shape, F32),
        compiler_params=_params("parallel"),
        name=f"post_l{layer}",
    )(h, om, od, p3, gains, gg, wo, wg, wu, wd, wp, wgate)


def _prep_weights(w_in, w_q_up, w_kv_up):
    depth = w_in.shape[0]
    c1 = Q_LORA + KV_LORA
    c2 = c1 + MLA_ROPE
    zeros = lambda n: jnp.zeros((depth, D_MODEL, n), w_in.dtype)
    win = jnp.concatenate(
        [w_in[..., :c2], zeros(LANES - MLA_ROPE), w_in[..., c2:]], axis=-1).astype(BF16)
    wq = w_q_up.reshape(depth, Q_LORA, MLA_HEADS, MLA_QK)
    wq = jnp.concatenate([wq[..., :MLA_NOPE].reshape(depth, Q_LORA, MLA_HEADS * MLA_NOPE),
                          wq[..., MLA_NOPE:].reshape(depth, Q_LORA, MLA_HEADS * MLA_ROPE)],
                         axis=-1).astype(BF16)
    wkv = w_kv_up.reshape(depth, KV_LORA, MLA_HEADS, MLA_NOPE + MLA_V)
    wk = jnp.pad(wkv[..., :MLA_NOPE], ((0, 0), (0, 0), (0, 0), (0, LANES - MLA_NOPE)))
    wk = wk.reshape(depth, KV_LORA, MLA_HEADS * LANES)
    wv = wkv[..., MLA_NOPE:].reshape(depth, KV_LORA, MLA_WIDTH)
    wkv = jnp.concatenate([wk, wv], axis=-1).astype(BF16)
    return win, wq, wkv


def _rope_tables(positions):
    pos = positions.astype(F32)[..., None]
    lane = jnp.arange(LANES)

    def patterns(dim):
        half = dim // 2
        inv = ROPE_THETA ** (-jnp.arange(0, dim, 2, dtype=F32) / dim)
        ang = pos * inv[lane % half]
        return jnp.cos(ang), jnp.sin(ang)

    return jnp.stack(patterns(MLA_ROPE) + patterns(DIL_HEAD_DIM), axis=0)


def kernel(x, p, positions, norm_gains, w_in, q_norm, w_q_up, kv_norm, w_kv_up, group_out_norm, w_out,
           ffn_gate, ffn_up, ffn_down, w_ple, w_ple_gate):
    b, s, d = x.shape
    depth = norm_gains.shape[0]
    n = b * s
    assert s % DIL_CHUNK == 0 and s % MLA_T == 0 and s % TM == 0 and d == D_MODEL

    tab = _rope_tables(positions)
    win, wq, wkv = _prep_weights(w_in, w_q_up, w_kv_up)
    wo = w_out.astype(BF16)
    wg, wu, wd = ffn_gate.astype(BF16), ffn_up.astype(BF16), ffn_down.astype(BF16)
    wp, wpg = w_ple.astype(BF16), w_ple_gate.astype(BF16)
    p3 = p.reshape(depth, n, PLE_DIM)
    q_norm = q_norm.reshape(depth, 1, Q_LORA)
    kv_norm = kv_norm.reshape(depth, 1, KV_LORA)
    group_out_norm = group_out_norm.reshape(depth, 1, D_MODEL)

    h = x.reshape(n, d)
    for i in range(depth):
        h = _ffn(h, norm_gains, wg, wu, wd, i)
        proj = _inproj(h.reshape(b, s, d), norm_gains, win, q_norm, wq, kv_norm, wkv, tab, i)
        om = _mla(*proj[:3], i)
        od = _dilated(proj[3:6], proj[6:9], proj[9:12], i)
        h = _post(h, om.reshape(n, MLA_WIDTH), od.reshape(n, DIL_WIDTH), p3, norm_gains, group_out_norm,
                  wo, wg, wu, wd, wp, wpg, i)
    return h.reshape(b, s, d)
```
